```python
import jax, jax.numpy as jnp
from jax import lax
import numpy as np

D_MODEL = 1024
BATCH = 8
SEQ = 2048
DEPTH = 2

HEAD_DIM = 64
N_SB = 6
N_DSA = 6
N_IDX = 8
IDX_DIM = 64
TOPK_MAX = 256
N_HG = 4
HG_DK = 128
HG_DV = 64
D_FF = 4 * D_MODEL
ROPE_THETA = 500000.0
ROT_DIM = HEAD_DIM // 4
Q_BLOCK = 128
HG_CHUNK = 64
EPS = 1e-6
F_MIN = 1e-12
NEG = -1e30
W_SB = N_SB * HEAD_DIM
W_DSA = N_DSA * HEAD_DIM
W_HF = N_HG * HG_DK
W_HV = N_HG * HG_DV
IDX_SCALE = (IDX_DIM * N_IDX) ** -0.5
SPLITS = (W_SB, W_SB, W_SB,
          W_DSA, HEAD_DIM, HEAD_DIM,
          N_IDX * IDX_DIM, IDX_DIM, N_IDX,
          W_HF, W_HF, W_HV, W_HV,
          D_MODEL, D_MODEL, D_MODEL)
D_IN = sum(SPLITS)

kernel_name = 'hybrid_sb_dsa_hgrn2_block'


def _split_points():
    return [int(v) for v in np.cumsum(np.array(SPLITS))[:-1]]


def rms_norm(x, gain):
    xf = x.astype(jnp.float32)
    y = xf * lax.rsqrt(jnp.mean(xf * xf, axis=-1, keepdims=True) + EPS)
    return (y * gain.astype(jnp.float32)).astype(x.dtype)


def partial_rope(x, pos):
    half = ROT_DIM // 2
    inv = ROPE_THETA ** (-(jnp.arange(half, dtype=jnp.float32) * 2.0) / ROT_DIM)
    ang = pos.astype(jnp.float32)[:, None] * inv[None, :]
    cos = jnp.cos(ang)[None, :, None, :]
    sin = jnp.sin(ang)[None, :, None, :]
    xr = x[..., :ROT_DIM].astype(jnp.float32)
    x1, x2 = xr[..., :half], xr[..., half:]
    rot = jnp.concatenate([x1 * cos - x2 * sin, x2 * cos + x1 * sin], axis=-1).astype(x.dtype)
    return jnp.concatenate([rot, x[..., ROT_DIM:]], axis=-1)


def stick_breaking_attention(q, k, v):
    B, T, H, D = q.shape
    scale = D ** -0.5
    outs = []
    for blk in range(T // Q_BLOCK):
        q0, q1 = blk * Q_BLOCK, (blk + 1) * Q_BLOCK
        z = jnp.einsum('bqhd,bkhd->bhqk', q[:, q0:q1], k[:, :q1]).astype(jnp.float32) * scale
        t_pos = q0 + jnp.arange(Q_BLOCK)[:, None]
        s_pos = jnp.arange(q1)[None, :]
        past = s_pos < t_pos
        log_1m = jnp.where(past, jax.nn.log_sigmoid(-z), 0.0)
        later = lax.cumsum(log_1m, axis=3, reverse=True) - log_1m
        log_a = jnp.where(past, jax.nn.log_sigmoid(z) + later, NEG)
        a = jnp.exp(log_a)
        outs.append(jnp.einsum('bhqk,bkhd->bqhd', a.astype(v.dtype), v[:, :q1]))
    return jnp.concatenate(outs, axis=1)


def dsa_attention(q, k, v, q_idx, k_idx, w_idx):
    B, T, H, D = q.shape
    k_top = min(TOPK_MAX, T // 4)
    gather = jax.vmap(lambda arr, idx: arr[idx])
    outs = []
    for blk in range(T // Q_BLOCK):
        q0, q1 = blk * Q_BLOCK, (blk + 1) * Q_BLOCK
        kl = min(T, max(q1, k_top))
        rel = jax.nn.relu(jnp.einsum('bqjd,bkd->bqjk', q_idx[:, q0:q1], k_idx[:, :kl]).astype(jnp.float32))
        score = jnp.einsum('bqj,bqjk->bqk', w_idx[:, q0:q1].astype(jnp.float32) * IDX_SCALE, rel)
        t_pos = q0 + jnp.arange(Q_BLOCK)[:, None]
        s_pos = jnp.arange(kl)[None, :]
        score = jnp.where(s_pos <= t_pos, score, NEG)
        top_score, top_idx = lax.top_k(score, k_top)
        k_sel = gather(k, top_idx)
        v_sel = gather(v, top_idx)
        logits = jnp.einsum('bqhd,bqkd->bhqk', q[:, q0:q1], k_sel).astype(jnp.float32) * (D ** -0.5)
        valid = (top_score > 0.5 * NEG)[:, None]
        logits = jnp.where(valid, logits, NEG)
        p = jax.nn.softmax(logits, axis=-1)
        outs.append(jnp.einsum('bhqk,bqkd->bqhd', p.astype(v.dtype), v_sel))
    return jnp.concatenate(outs, axis=1)


def hgrn2(q, f_pre, inp, lb):
    B, T, H, DK = q.shape
    DV = inp.shape[-1]
    C = HG_CHUNK
    N = T // C
    q = jax.nn.silu(q.astype(jnp.float32))
    lb = lb.astype(jnp.float32)
    f = lb + (1.0 - lb) * jax.nn.sigmoid(f_pre.astype(jnp.float32))
    log_f = jnp.log(jnp.maximum(f, F_MIN))
    k = 1.0 - f

    def to_chunks(a):
        return a.reshape(B, N, C, H, a.shape[-1]).transpose(1, 0, 3, 2, 4)

    qc, kc, vc, gc = (to_chunks(a) for a in (q, k, inp.astype(jnp.float32), log_f))
    causal = jnp.tril(jnp.ones((C, C), dtype=bool))[:, :, None]

    def step(S, xs):
        qt, kt, vt, gt = xs
        b = jnp.cumsum(gt, axis=2)
        diff = b[:, :, :, None, :] - b[:, :, None, :, :]
        decay = jnp.exp(jnp.where(causal, diff, NEG))
        att = jnp.einsum('bhtk,bhsk,bhtsk->bhts', qt, kt, decay)
        o = jnp.einsum('bhts,bhsv->bhtv', att, vt) + jnp.einsum('bhtk,bhkv->bhtv', qt * jnp.exp(b), S)
        b_last = b[:, :, -1:, :]
        S = S * jnp.exp(b_last[:, :, 0, :, None]) + jnp.einsum('bhsk,bhsv->bhkv', kt * jnp.exp(b_last - b), vt)
        return S, o

    S0 = jnp.zeros((B, H, DK, DV), jnp.float32)
    _, o = lax.scan(step, S0, (qc, kc, vc, gc))
    return o.transpose(1, 0, 3, 2, 4).reshape(B, T, H, DV).astype(inp.dtype)


def hybrid_mixer(h, w_in_l, qn, kn, lb, onorm, w_sb_l, w_dsa_l, w_hg_l, w_out_l):
    B, T, _ = h.shape
    pos = jnp.arange(T)
    (sq, sk, sv, dq, dk, dv, iq, ik, iw, hq, hf, hi, hg,
     g_sb, g_dsa, g_hg) = jnp.split(h @ w_in_l, _split_points(), axis=-1)
    heads = lambda a, n: a.reshape(B, T, n, -1)
    y_sb = stick_breaking_attention(heads(sq, N_SB), heads(sk, N_SB), heads(sv, N_SB)).reshape(B, T, W_SB) @ w_sb_l
    q = partial_rope(rms_norm(heads(dq, N_DSA), qn), pos)
    k = partial_rope(rms_norm(dk[:, :, None, :], kn), pos)[:, :, 0]
    q_i = partial_rope(heads(iq, N_IDX), pos)
    k_i = partial_rope(ik[:, :, None, :], pos)[:, :, 0]
    y_dsa = dsa_attention(q, k, dv, q_i, k_i, iw).reshape(B, T, W_DSA) @ w_dsa_l
    o = hgrn2(heads(hq, N_HG), heads(hf, N_HG), heads(hi, N_HG), lb.reshape(N_HG, HG_DK))
    o = rms_norm(o, onorm) * jax.nn.silu(heads(hg, N_HG))
    y_hg = o.reshape(B, T, W_HV) @ w_hg_l
    mixed = jax.nn.sigmoid(g_sb) * y_sb + jax.nn.sigmoid(g_dsa) * y_dsa + jax.nn.sigmoid(g_hg) * y_hg
    return mixed @ w_out_l


def setup_inputs(seed: int = 0) -> dict:
    key = jax.random.key(seed)
    ks = jax.random.split(key, 16)
    f32 = jnp.float32
    res = (2 * DEPTH) ** -0.5

    def nrm(k, shape, fan_in, g=1.0):
        return jax.random.normal(k, shape, f32) * (g * fan_in ** -0.5)

    def gain(k, shape):
        return 1.0 + 0.02 * jax.random.normal(k, shape, f32)

    return {
        'x': jax.random.normal(ks[0], (BATCH, SEQ, D_MODEL), f32),
        'norm_mix': gain(ks[1], (DEPTH, D_MODEL)),
        'w_in': nrm(ks[2], (DEPTH, D_MODEL, D_IN), D_MODEL),
        'qn_dsa': gain(ks[3], (DEPTH, HEAD_DIM)),
        'kn_dsa': gain(ks[4], (DEPTH, HEAD_DIM)),
        'hgrn_lb': 0.5 * jax.random.normal(ks[5], (DEPTH, W_HF), f32),
        'hgrn_onorm': gain(ks[6], (DEPTH, HG_DV)),
        'w_br_sb': nrm(ks[7], (DEPTH, W_SB, D_MODEL), W_SB),
        'w_br_dsa': nrm(ks[8], (DEPTH, W_DSA, D_MODEL), W_DSA),
        'w_br_hgrn': nrm(ks[9], (DEPTH, W_HV, D_MODEL), W_HV),
        'w_out': nrm(ks[10], (DEPTH, D_MODEL, D_MODEL), D_MODEL, res),
        'norm_mlp': gain(ks[11], (DEPTH, D_MODEL)),
        'w_up': nrm(ks[12], (DEPTH, D_MODEL, D_FF), D_MODEL),
        'w_down': nrm(ks[13], (DEPTH, D_FF, D_MODEL), D_FF, res),
    }


def reference(x, norm_mix, w_in, qn_dsa, kn_dsa, hgrn_lb, hgrn_onorm, w_br_sb, w_br_dsa,
              w_br_hgrn, w_out, norm_mlp, w_up, w_down):
    p_lb = jax.nn.softmax(hgrn_lb.astype(jnp.float32), axis=0)
    lbs = jnp.cumsum(p_lb, axis=0) - p_lb[0:1]
    for l in range(DEPTH):
        h = rms_norm(x, norm_mix[l])
        x = x + hybrid_mixer(h, w_in[l], qn_dsa[l], kn_dsa[l], lbs[l], hgrn_onorm[l],
                             w_br_sb[l], w_br_dsa[l], w_br_hgrn[l], w_out[l])
        h2 = rms_norm(x, norm_mlp[l])
        x = x + jnp.square(jax.nn.relu(h2 @ w_up[l])) @ w_down[l]
    return x
```

```python
import functools

import numpy as np
import jax
import jax.numpy as jnp
from jax import lax
from jax.experimental import pallas as pl
from jax.experimental.pallas import tpu as pltpu

f32 = jnp.float32
bf16 = jnp.bfloat16

HEAD_DIM = 64
N_SB = 6
N_DSA = 6
N_IDX = 8
IDX_DIM = 64
TOPK_MAX = 256
N_HG = 4
HG_DK = 128
HG_DV = 64
ROPE_THETA = 500000.0
ROT_DIM = HEAD_DIM // 4
EPS = 1e-6
F_MIN = 1e-12
NEG = -1e30
W_SB = N_SB * HEAD_DIM
W_DSA = N_DSA * HEAD_DIM
W_HF = N_HG * HG_DK
W_HV = N_HG * HG_DV
IDX_SCALE = (IDX_DIM * N_IDX) ** -0.5
ATT_SCALE = HEAD_DIM ** -0.5

_IN_NAMES = ("sq", "sk", "sv", "dq", "dk", "dv", "iq", "ik", "iw",
             "hq", "hf", "hi", "hg", "g_sb", "g_dsa", "g_hg")

LANES = 128
QB = 128
HG_CHUNK = 64
HG_SUB = 16
HG_ROWS = 128

COL_G = 0
COL_HQ = 3072
COL_HF = 3584
COL_IQ = 4096
COL_HI = 4608
COL_HG = 4864
COL_DKV = 5120
COL_IK = 5248
COL_SQ = 5376
COL_SK = 5760
COL_SV = 6144
COL_DQ = 6528
COL_IW = 6912
NP = 7168


def _dot(a, b):
    return jnp.dot(a, b, preferred_element_type=f32)


def _dot_nt(a, b):
    return lax.dot_general(a, b, (((1,), (1,)), ((), ())), preferred_element_type=f32)


def _split3(x):
    hi = x.astype(bf16)
    r = x - hi.astype(f32)
    mid = r.astype(bf16)
    lo = (r - mid.astype(f32)).astype(bf16)
    return hi, mid, lo


def _dot3_right(x, m):
    hi, mid, lo = _split3(x)
    return _dot(hi, m) + _dot(mid, m) + _dot(lo, m)


def _dot3_left(m, x):
    hi, mid, lo = _split3(x)
    return _dot(m, hi) + _dot(m, mid) + _dot(m, lo)


def _group_mean_matrix(width, group):
    i = lax.broadcasted_iota(jnp.int32, (width, width), 0)
    j = lax.broadcasted_iota(jnp.int32, (width, width), 1)
    return jnp.where((i // group) == (j // group), 1.0 / group, 0.0).astype(bf16)


def _rope(x, cos, sin):
    half = ROT_DIM // 2
    outs = []
    for c in range(x.shape[1] // LANES):
        xc = x[:, c * LANES:(c + 1) * LANES]
        lane = lax.broadcasted_iota(jnp.int32, xc.shape, 1)
        partner = jnp.where((lane % HEAD_DIM) < half,
                            pltpu.roll(xc, LANES - half, axis=1),
                            pltpu.roll(xc, half, axis=1))
        outs.append(xc * cos[:, c * LANES:(c + 1) * LANES] + partner * sin[:, c * LANES:(c + 1) * LANES])
    return outs[0] if len(outs) == 1 else jnp.concatenate(outs, axis=1)


def _softplus(z):
    return jnp.maximum(z, 0.0) + jnp.log1p(jnp.exp(-jnp.abs(z)))


def _sigmoid(z):
    return 1.0 / (1.0 + jnp.exp(-z))


def _inproj_kernel(x_ref, g_ref, w_ref, o_ref, h_ref):
    @pl.when(pl.program_id(1) == 0)
    def _():
        x = x_ref[...]
        ms = jnp.mean(x * x, axis=-1, keepdims=True)
        h_ref[...] = (x * lax.rsqrt(ms + EPS) * g_ref[...]).astype(bf16)

    o_ref[...] = _dot(h_ref[...], w_ref[...])


def _inproj(x2d, gain, w):
    m, d = x2d.shape
    n = w.shape[1]
    tm, tn = 512, 1024
    return pl.pallas_call(
        _inproj_kernel,
        grid=(m // tm, n // tn),
        in_specs=[pl.BlockSpec((tm, d), lambda i, j: (i, 0)),
                  pl.BlockSpec((1, d), lambda i, j: (0, 0)),
                  pl.BlockSpec((d, tn), lambda i, j: (0, j))],
        out_specs=pl.BlockSpec((tm, tn), lambda i, j: (i, j)),
        out_shape=jax.ShapeDtypeStruct((m, n), f32),
        scratch_shapes=[pltpu.VMEM((tm, d), bf16)],
        compiler_params=pltpu.CompilerParams(dimension_semantics=("parallel", "arbitrary")),
        name="inproj",
    )(x2d, gain, w)


def _sb_kernel(q_ref, k_ref, v_ref, o_ref, kb_ref, vb_ref):
    qi = pl.program_id(2)

    @pl.when(qi == 0)
    def _():
        kb_ref[...] = k_ref[0].astype(bf16)
        vb_ref[...] = v_ref[0].astype(bf16)

    q = q_ref[0]
    lane = lax.broadcasted_iota(jnp.int32, (QB, LANES), 1)
    row = lax.broadcasted_iota(jnp.int32, (QB, LANES), 0)
    jj = lax.broadcasted_iota(jnp.int32, (LANES, 2 * LANES), 0)
    ss = lax.broadcasted_iota(jnp.int32, (LANES, 2 * LANES), 1)
    later_and_sum = jnp.where((ss >= LANES) | (jj > ss), 1.0, 0.0).astype(bf16)

    outs = []
    for hh in range(2):
        qh = (jnp.where((lane // HEAD_DIM) == hh, q, 0.0) * ATT_SCALE).astype(bf16)

        def body(j, carry, qh=qh):
            c, acc = carry
            kb = qi - j
            off = pl.multiple_of(kb * QB, QB)
            kblk = kb_ref[pl.ds(off, QB), :]
            vblk = vb_ref[pl.ds(off, QB), :]
            z = _dot_nt(qh, kblk)
            past = (lane + kb * QB) < (row + qi * QB)
            sp = _softplus(z)
            lm = jnp.where(past, -sp, 0.0)
            r = _dot3_right(lm, later_and_sum)
            log_a = z - sp + r[:, :LANES] + c
            a = jnp.exp(jnp.where(past, log_a, NEG))
            acc = acc + _dot(a.astype(bf16), vblk)
            return c + r[:, LANES:], acc

        zero = jnp.zeros((QB, LANES), f32)
        _, acc = lax.fori_loop(0, qi + 1, body, (zero, zero))
        outs.append(acc)
    o_ref[0] = jnp.where(lane < HEAD_DIM, outs[0], outs[1]).astype(o_ref.dtype)


def _sb_attention(p3):
    b, t, _ = p3.shape
    npair = N_SB // 2
    cq, ck, cv = COL_SQ // LANES, COL_SK // LANES, COL_SV // LANES
    return pl.pallas_call(
        _sb_kernel,
        grid=(b, npair, t // QB),
        in_specs=[pl.BlockSpec((1, QB, LANES), lambda bi, p, qi: (bi, qi, cq + p)),
                  pl.BlockSpec((1, t, LANES), lambda bi, p, qi: (bi, 0, ck + p)),
                  pl.BlockSpec((1, t, LANES), lambda bi, p, qi: (bi, 0, cv + p))],
        out_specs=pl.BlockSpec((1, QB, LANES), lambda bi, p, qi: (bi, qi, p)),
        out_shape=jax.ShapeDtypeStruct((b, t, W_SB), bf16),
        scratch_shapes=[pltpu.VMEM((t, LANES), bf16), pltpu.VMEM((t, LANES), bf16)],
        compiler_params=pltpu.CompilerParams(
            dimension_semantics=("parallel", "parallel", "arbitrary")),
        name="sb_attention",
    )(p3, p3, p3)


def _float_key(x):
    bits = pltpu.bitcast(x, jnp.int32)
    return jnp.where(bits < 0, bits ^ jnp.int32(0x7FFFFFFF), bits)


def _np_float_key(v):
    bits = int(np.array(v, np.float32).view(np.int32))
    return bits ^ 0x7FFFFFFF if bits < 0 else bits


_KEY_HALF_NEG = _np_float_key(0.5 * NEG)
_INT_MIN = -2 ** 31


def _dsa_kernel(dq_ref, kv_ref, iq_ref, ik_ref, iw_ref, cq_ref, sq_ref, ck_ref, sk_ref,
                qn_ref, kn_ref, o_ref, kn2_ref, v2_ref, ki2_ref, key_ref, bias_ref, *, k_top):
    qi = pl.program_id(1)
    t = kv_ref.shape[1]

    @pl.when(qi == 0)
    def _():
        lane = lax.broadcasted_iota(jnp.int32, (t, LANES), 1)
        kv = kv_ref[0]
        swapped = pltpu.roll(kv, HEAD_DIM, axis=1)
        k2 = jnp.where(lane < HEAD_DIM, kv, swapped)
        v2_ref[...] = jnp.where(lane < HEAD_DIM, swapped, kv).astype(bf16)
        ms = jnp.mean(k2 * k2, axis=-1, keepdims=True)
        kn2 = k2 * lax.rsqrt(ms + EPS) * kn_ref[...]
        kn2_ref[...] = _rope(kn2, ck_ref[...], sk_ref[...]).astype(bf16)
        ik = ik_ref[0]
        ik2 = jnp.where(lane < HEAD_DIM, ik, pltpu.roll(ik, HEAD_DIM, axis=1))
        ki2_ref[...] = _rope(ik2, ck_ref[...], sk_ref[...]).astype(bf16)

    cos = cq_ref[...]
    sin = sq_ref[...]
    lane = lax.broadcasted_iota(jnp.int32, (QB, LANES), 1)
    half_id = lane // HEAD_DIM

    iq = _rope(iq_ref[0], cos, sin)
    iw = iw_ref[0] * IDX_SCALE
    score = jnp.zeros((QB, t), f32)
    for j in range(N_IDX):
        pair = iq[:, (j // 2) * LANES:(j // 2 + 1) * LANES]
        qj = jnp.where(half_id == (j % 2), pair, 0.0).astype(bf16)
        rel = jnp.maximum(_dot_nt(qj, ki2_ref[...]), 0.0)
        score = score + iw[:, j:j + 1] * rel
    col = lax.broadcasted_iota(jnp.int32, (QB, t), 1)
    row = lax.broadcasted_iota(jnp.int32, (QB, t), 0) + qi * QB
    score = jnp.where(col <= row, score + 0.0, NEG)
    key_ref[...] = _float_key(score)

    kf = float(k_top)

    def count_ge(trial):
        return jnp.sum(jnp.where(key_ref[...] >= trial, 1.0, 0.0), axis=-1, keepdims=True)

    lo = jnp.where(count_ge(jnp.zeros((QB, 1), jnp.int32)) >= kf,
                   jnp.int32(0), jnp.int32(_INT_MIN))

    def bit_step(i, lo):
        trial = lo | jnp.left_shift(jnp.int32(1), 30 - i)
        return jnp.where(count_ge(trial) >= kf, trial, lo)

    thr = lax.fori_loop(0, 31, bit_step, lo)

    keyv = key_ref[...]
    adm = keyv > _KEY_HALF_NEG
    cnt_gt = jnp.sum(jnp.where(keyv > thr, 1.0, 0.0), axis=-1, keepdims=True)
    n_eq = jnp.sum(jnp.where(keyv == thr, 1.0, 0.0), axis=-1, keepdims=True)
    need = kf - cnt_gt
    bias_ref[...] = jnp.where(adm & (keyv >= thr), 0.0, NEG)

    tie_overflow = jnp.where((n_eq > need) & (thr > _KEY_HALF_NEG), 1.0, 0.0)

    @pl.when(jnp.max(tie_overflow) > 0.0)
    def _():
        i = lax.broadcasted_iota(jnp.int32, (LANES, LANES), 0)
        j = lax.broadcasted_iota(jnp.int32, (LANES, LANES), 1)
        before = jnp.where(i < j, 1.0, 0.0).astype(bf16)
        seen = jnp.zeros((QB, 1), f32)
        for cb in range(t // LANES):
            kb = key_ref[:, cb * LANES:(cb + 1) * LANES]
            eq = jnp.where(kb == thr, 1.0, 0.0)
            rank = _dot(eq.astype(bf16), before) + seen
            seen = seen + jnp.sum(eq, axis=-1, keepdims=True)
            sel = (kb > _KEY_HALF_NEG) & ((kb > thr) | ((kb == thr) & (rank < need)))
            bias_ref[:, cb * LANES:(cb + 1) * LANES] = jnp.where(sel, 0.0, NEG)

    x = dq_ref[0]
    ms = _dot3_right(x * x, _group_mean_matrix(W_DSA, HEAD_DIM))
    qn = x * lax.rsqrt(ms + EPS) * qn_ref[...]
    qn = _rope(qn, cos[:, :W_DSA], sin[:, :W_DSA]) * ATT_SCALE
    bias = bias_ref[...]
    for pair in range(N_DSA // 2):
        qp = qn[:, pair * LANES:(pair + 1) * LANES]
        res = []
        for hh in range(2):
            qh = jnp.where(half_id == hh, qp, 0.0).astype(bf16)
            logits = _dot_nt(qh, kn2_ref[...]) + bias
            m = jnp.max(logits, axis=-1, keepdims=True)
            p = jnp.exp(logits - m)
            l = jnp.sum(p, axis=-1, keepdims=True)
            res.append(_dot(p.astype(bf16), v2_ref[...]) / l)
        o_ref[0, :, pair * LANES:(pair + 1) * LANES] = jnp.where(
            lane < HEAD_DIM, res[0], res[1]).astype(o_ref.dtype)


def _dsa_attention(p3, cos_t, sin_t, qn_gain, kn_gain):
    b, t, _ = p3.shape
    k_top = min(TOPK_MAX, t // 4)
    return pl.pallas_call(
        functools.partial(_dsa_kernel, k_top=k_top),
        grid=(b, t // QB),
        in_specs=[
            pl.BlockSpec((1, QB, W_DSA), lambda bi, qi: (bi, qi, COL_DQ // W_DSA)),
            pl.BlockSpec((1, t, LANES), lambda bi, qi: (bi, 0, COL_DKV // LANES)),
            pl.BlockSpec((1, QB, N_IDX * IDX_DIM), lambda bi, qi: (bi, qi, COL_IQ // (N_IDX * IDX_DIM))),
            pl.BlockSpec((1, t, LANES), lambda bi, qi: (bi, 0, COL_IK // LANES)),
            pl.BlockSpec((1, QB, LANES), lambda bi, qi: (bi, qi, COL_IW // LANES)),
            pl.BlockSpec((QB, 4 * LANES), lambda bi, qi: (qi, 0)),
            pl.BlockSpec((QB, 4 * LANES), lambda bi, qi: (qi, 0)),
            pl.BlockSpec((t, LANES), lambda bi, qi: (0, 0)),
            pl.BlockSpec((t, LANES), lambda bi, qi: (0, 0)),
            pl.BlockSpec((1, W_DSA), lambda bi, qi: (0, 0)),
            pl.BlockSpec((1, LANES), lambda bi, qi: (0, 0)),
        ],
        out_specs=pl.BlockSpec((1, QB, W_DSA), lambda bi, qi: (bi, qi, 0)),
        out_shape=jax.ShapeDtypeStruct((b, t, W_DSA), bf16),
        scratch_shapes=[pltpu.VMEM((t, LANES), bf16), pltpu.VMEM((t, LANES), bf16),
                        pltpu.VMEM((t, LANES), bf16), pltpu.VMEM((QB, t), jnp.int32),
                        pltpu.VMEM((QB, t), f32)],
        compiler_params=pltpu.CompilerParams(
            dimension_semantics=("parallel", "arbitrary"), vmem_limit_bytes=48 * 2 ** 20),
        name="dsa_attention",
    )(p3, p3, p3, p3, p3, cos_t, sin_t, cos_t, sin_t, qn_gain, kn_gain)


def _hgrn_kernel(hq_ref, hf_ref, hi_ref, vt_ref, lb_ref, o_ref, st_ref):
    @pl.when(pl.program_id(1) == 0)
    def _():
        st_ref[...] = jnp.zeros_like(st_ref)

    c = HG_CHUNK
    nsub = c // HG_SUB
    ti = lax.broadcasted_iota(jnp.int32, (c, c), 0)
    si = lax.broadcasted_iota(jnp.int32, (c, c), 1)
    cum = jnp.where(si <= ti, 1.0, 0.0).astype(bf16)
    same_sub = (ti // HG_SUB) == (si // HG_SUB)
    later_sub = (ti // HG_SUB) > (si // HG_SUB)
    delta = ti - si
    rsub = lax.broadcasted_iota(jnp.int32, (c, HG_DK), 0) // HG_SUB

    for sub in range(HG_ROWS // c):
        r0 = sub * c
        for h in range(N_HG):
            qraw = hq_ref[0, r0:r0 + c, h * HG_DK:(h + 1) * HG_DK]
            fpre = hf_ref[0, r0:r0 + c, h * HG_DK:(h + 1) * HG_DK]
            lb = lb_ref[:, h * HG_DK:(h + 1) * HG_DK]
            v = hi_ref[0, r0:r0 + c, h * HG_DV:(h + 1) * HG_DV]
            vt = vt_ref[0, h, sub]
            q = qraw * _sigmoid(qraw)
            f = lb + (1.0 - lb) * _sigmoid(fpre)
            g = jnp.log(jnp.maximum(f, F_MIN))
            kk = 1.0 - f
            b = _dot3_left(cum, g)
            b_last = b[c - 1:c, :]
            ends = [b[(j + 1) * HG_SUB - 1:(j + 1) * HG_SUB, :] for j in range(nsub)]
            b_end = jnp.concatenate([jnp.broadcast_to(e, (HG_SUB, HG_DK)) for e in ends], axis=0)

            k_rel = kk * jnp.exp(b_end - b)
            k_cat = jnp.concatenate([jnp.where(rsub == j, k_rel, 0.0) for j in range(nsub - 1)], axis=1)
            q_cat = jnp.concatenate([q * jnp.exp(jnp.minimum(b - ends[j], 0.0)) for j in range(nsub - 1)],
                                    axis=1)
            att = jnp.where(later_sub, _dot_nt(q_cat.astype(bf16), k_cat.astype(bf16)), 0.0)

            for d in range(HG_SUB):
                if d == 0:
                    term = q * kk
                else:
                    kd = pltpu.roll(kk, d, axis=0)
                    bd = pltpu.roll(b, d, axis=0)
                    term = q * kd * jnp.exp(jnp.minimum(b - bd, 0.0))
                s = jnp.sum(term, axis=-1, keepdims=True)
                att = att + jnp.where(same_sub & (delta == d), s, 0.0)

            st = st_ref[h]
            o = _dot(att.astype(bf16), v.astype(bf16)) + _dot_nt((q * jnp.exp(b)).astype(bf16), st.astype(bf16))
            o_ref[0, h, r0:r0 + c, :] = o
            k_hat = kk * jnp.exp(b_last - b)
            st_ref[h] = st * jnp.exp(b_last) + _dot(vt.astype(bf16), k_hat.astype(bf16))


def _hgrn(p3, vt, lb):
    b, t, _ = p3.shape
    per = HG_ROWS // HG_CHUNK
    return pl.pallas_call(
        _hgrn_kernel,
        grid=(b, t // HG_ROWS),
        in_specs=[pl.BlockSpec((1, HG_ROWS, W_HF), lambda bi, i: (bi, i, COL_HQ // W_HF)),
                  pl.BlockSpec((1, HG_ROWS, W_HF), lambda bi, i: (bi, i, COL_HF // W_HF)),
                  pl.BlockSpec((1, HG_ROWS, W_HV), lambda bi, i: (bi, i, COL_HI // W_HV)),
                  pl.BlockSpec((1, N_HG, per, HG_DV, HG_CHUNK), lambda bi, i: (bi, 0, i, 0, 0)),
                  pl.BlockSpec((1, W_HF), lambda bi, i: (0, 0))],
        out_specs=pl.BlockSpec((1, N_HG, HG_ROWS, HG_DV), lambda bi, i: (bi, 0, i, 0)),
        out_shape=jax.ShapeDtypeStruct((b, N_HG, t, HG_DV), f32),
        scratch_shapes=[pltpu.VMEM((N_HG, HG_DV, HG_DK), f32)],
        compiler_params=pltpu.CompilerParams(dimension_semantics=("parallel", "arbitrary")),
        name="hgrn2",
    )(p3, p3, p3, vt, lb)


def _merge_kernel(x_ref, sb_ref, dsa_ref, hg_ref, gs_ref, gd_ref, gh_ref, og_ref, on_ref,
                  wsb_ref, wdsa_ref, whg_ref, wout_ref, o_ref):
    o = hg_ref[...]
    ms = _dot3_right(o * o, _group_mean_matrix(W_HV, HG_DV))
    gate = og_ref[...]
    og = o * lax.rsqrt(ms + EPS) * on_ref[...] * (gate * _sigmoid(gate))
    y_hg = _dot(og.astype(bf16), whg_ref[...])
    y_sb = _dot(sb_ref[...], wsb_ref[...])
    y_dsa = _dot(dsa_ref[...], wdsa_ref[...])
    mixed = (_sigmoid(gs_ref[...]) * y_sb + _sigmoid(gd_ref[...]) * y_dsa + _sigmoid(gh_ref[...]) * y_hg)
    o_ref[...] = x_ref[...] + _dot(mixed.astype(bf16), wout_ref[...])


def _merge(x2d, att_sb, att_dsa, o_hg, p2d, onorm, w_sb, w_dsa, w_hg, w_out):
    m, d = x2d.shape
    tm = 512
    row = lambda w: pl.BlockSpec((tm, w), lambda i: (i, 0))
    full = lambda a: pl.BlockSpec(a.shape, lambda i: (0, 0))
    return pl.pallas_call(
        _merge_kernel,
        grid=(m // tm,),
        in_specs=[row(d), row(W_SB), row(W_DSA), row(W_HV),
                  pl.BlockSpec((tm, d), lambda i: (i, COL_G // d)),
                  pl.BlockSpec((tm, d), lambda i: (i, COL_G // d + 1)),
                  pl.BlockSpec((tm, d), lambda i: (i, COL_G // d + 2)),
                  pl.BlockSpec((tm, W_HV), lambda i: (i, COL_HG // W_HV)),
                  full(onorm), full(w_sb), full(w_dsa), full(w_hg), full(w_out)],
        out_specs=row(d),
        out_shape=jax.ShapeDtypeStruct((m, d), f32),
        compiler_params=pltpu.CompilerParams(
            dimension_semantics=("parallel",), vmem_limit_bytes=48 * 2 ** 20),
        name="merge_outproj",
    )(x2d, att_sb, att_dsa, o_hg, p2d, p2d, p2d, p2d, onorm, w_sb, w_dsa, w_hg, w_out)


def _mlp_kernel(x_ref, g_ref, wu_ref, wd_ref, o_ref, h_ref, acc_ref):
    j = pl.program_id(1)

    @pl.when(j == 0)
    def _():
        x = x_ref[...]
        ms = jnp.mean(x * x, axis=-1, keepdims=True)
        h_ref[...] = (x * lax.rsqrt(ms + EPS) * g_ref[...]).astype(bf16)
        acc_ref[...] = jnp.zeros_like(acc_ref)

    u = jnp.maximum(_dot(h_ref[...], wu_ref[...]), 0.0)
    acc_ref[...] += _dot((u * u).astype(bf16), wd_ref[...])

    @pl.when(j == pl.num_programs(1) - 1)
    def _():
        o_ref[...] = x_ref[...] + acc_ref[...]


def _mlp(x2d, gain, w_up, w_down):
    m, d = x2d.shape
    dff = w_up.shape[1]
    tm, tf = 1024, 512
    return pl.pallas_call(
        _mlp_kernel,
        grid=(m // tm, dff // tf),
        in_specs=[pl.BlockSpec((tm, d), lambda i, j: (i, 0)),
                  pl.BlockSpec((1, d), lambda i, j: (0, 0)),
                  pl.BlockSpec((d, tf), lambda i, j: (0, j)),
                  pl.BlockSpec((tf, d), lambda i, j: (j, 0))],
        out_specs=pl.BlockSpec((tm, d), lambda i, j: (i, 0)),
        out_shape=jax.ShapeDtypeStruct((m, d), f32),
        scratch_shapes=[pltpu.VMEM((tm, d), bf16), pltpu.VMEM((tm, d), f32)],
        compiler_params=pltpu.CompilerParams(
            dimension_semantics=("parallel", "arbitrary"), vmem_limit_bytes=48 * 2 ** 20),
        name="mlp",
    )(x2d, gain, w_up, w_down)


def _reorder_w_in(w):
    widths = (W_SB, W_SB, W_SB, W_DSA, HEAD_DIM, HEAD_DIM, N_IDX * IDX_DIM, IDX_DIM, N_IDX,
              W_HF, W_HF, W_HV, W_HV, w.shape[0], w.shape[0], w.shape[0])
    pts = np.concatenate([[0], np.cumsum(widths)])
    seg = {n: w[:, int(pts[i]):int(pts[i + 1])] for i, n in enumerate(_IN_NAMES)}
    zeros = lambda n: jnp.zeros((w.shape[0], n), w.dtype)
    cols = [seg["g_sb"], seg["g_dsa"], seg["g_hg"], seg["hq"], seg["hf"], seg["iq"], seg["hi"], seg["hg"],
            seg["dk"], seg["dv"], seg["ik"], zeros(LANES - IDX_DIM),
            seg["sq"], seg["sk"], seg["sv"], seg["dq"], seg["iw"], zeros(LANES - N_IDX),
            zeros(NP - COL_IW - LANES)]
    out = jnp.concatenate(cols, axis=1).astype(bf16)
    assert out.shape[1] == NP
    return out


def _rope_tables(t):
    half = ROT_DIM // 2
    inv = ROPE_THETA ** (-(np.arange(half, dtype=np.float32) * 2.0) / ROT_DIM)
    ang = jnp.arange(t, dtype=f32)[:, None] * jnp.asarray(inv, f32)[None, :]
    cos, sin = jnp.cos(ang), jnp.sin(ang)
    pad = HEAD_DIM - ROT_DIM
    cos_h = jnp.concatenate([cos, cos, jnp.ones((t, pad), f32)], axis=1)
    sin_h = jnp.concatenate([-sin, sin, jnp.zeros((t, pad), f32)], axis=1)
    reps = 4 * LANES // HEAD_DIM
    return jnp.tile(cos_h, (1, reps)), jnp.tile(sin_h, (1, reps))


def kernel(x, norm_mix, w_in, qn_dsa, kn_dsa, hgrn_lb, hgrn_onorm, w_br_sb, w_br_dsa, w_br_hgrn,
           w_out, norm_mlp, w_up, w_down):
    bsz, t, d = x.shape
    depth = w_in.shape[0]
    m = bsz * t
    assert d == 1024 and t % QB == 0 and m % 1024 == 0

    p_lb = jax.nn.softmax(hgrn_lb.astype(f32), axis=0)
    lbs = jnp.cumsum(p_lb, axis=0) - p_lb[0:1]
    cos_t, sin_t = _rope_tables(t)

    xf = x.reshape(m, d)
    for l in range(depth):
        p2d = _inproj(xf, norm_mix[l][None, :], _reorder_w_in(w_in[l]))
        p3 = p2d.reshape(bsz, t, NP)
        att_sb = _sb_attention(p3)
        att_dsa = _dsa_attention(p3, cos_t, sin_t, jnp.tile(qn_dsa[l], N_DSA)[None, :],
                                 jnp.tile(kn_dsa[l], 2)[None, :])
        hi = p3[:, :, COL_HI:COL_HI + W_HV]
        vt = jnp.transpose(hi.reshape(bsz, t // HG_CHUNK, HG_CHUNK, N_HG, HG_DV), (0, 3, 1, 4, 2))
        o_hg = _hgrn(p3, vt, lbs[l][None, :])
        o_hg = jnp.transpose(o_hg, (0, 2, 1, 3)).reshape(m, W_HV)
        xf = _merge(xf, att_sb.reshape(m, W_SB), att_dsa.reshape(m, W_DSA), o_hg, p2d,
                    jnp.tile(hgrn_onorm[l], N_HG)[None, :],
                    w_br_sb[l].astype(bf16), w_br_dsa[l].astype(bf16), w_br_hgrn[l].astype(bf16),
                    w_out[l].astype(bf16))
        xf = _mlp(xf, norm_mlp[l][None, :], w_up[l].astype(bf16), w_down[l].astype(bf16))
    return xf.reshape(bsz, t, d)
```

```python
import functools

import numpy as np
import jax
import jax.numpy as jnp
from jax import lax
from jax.experimental import pallas as pl
from jax.experimental.pallas import tpu as pltpu

f32 = jnp.float32
bf16 = jnp.bfloat16

HEAD_DIM = 64
N_SB = 6
N_DSA = 6
N_IDX = 8
IDX_DIM = 64
TOPK_MAX = 256
N_HG = 4
HG_DK = 128
HG_DV = 64
ROPE_THETA = 500000.0
ROT_DIM = HEAD_DIM // 4
EPS = 1e-6
F_MIN = 1e-12
NEG = -1e30
W_SB = N_SB * HEAD_DIM
W_DSA = N_DSA * HEAD_DIM
W_HF = N_HG * HG_DK
W_HV = N_HG * HG_DV
IDX_SCALE = (IDX_DIM * N_IDX) ** -0.5
ATT_SCALE = HEAD_DIM ** -0.5
LOG2_E = 1.4426950408889634
F32_EXP2_UNDERFLOW = -150.0

_IN_NAMES = ("sq", "sk", "sv", "dq", "dk", "dv", "iq", "ik", "iw",
             "hq", "hf", "hi", "hg", "g_sb", "g_dsa", "g_hg")

LANES = 128
QB = 128
HG_CHUNK = 64
HG_SUB = 16
HG_ROWS = 128

COL_G = 0
COL_HQ = 3072
COL_HF = 3584
COL_IQ = 4096
COL_HI = 4608
COL_HG = 4864
COL_DKV = 5120
COL_IK = 5248
COL_SQ = 5376
COL_SK = 5760
COL_SV = 6144
COL_DQ = 6528
COL_IW = 6912
NP = 7168


def _dot(a, b):
    return jnp.dot(a, b, preferred_element_type=f32)


def _dot_nt(a, b):
    return lax.dot_general(a, b, (((1,), (1,)), ((), ())), preferred_element_type=f32)


def _split3(x):
    hi = x.astype(bf16)
    r = x - hi.astype(f32)
    mid = r.astype(bf16)
    lo = (r - mid.astype(f32)).astype(bf16)
    return hi, mid, lo


def _dot3_right(x, m):
    hi, mid, lo = _split3(x)
    return _dot(hi, m) + _dot(mid, m) + _dot(lo, m)


def _dot2_right(x, m):
    hi = x.astype(bf16)
    lo = (x - hi.astype(f32)).astype(bf16)
    return _dot(hi, m) + _dot(lo, m)


def _dot3_left(m, x):
    hi, mid, lo = _split3(x)
    return _dot(m, hi) + _dot(m, mid) + _dot(m, lo)


def _group_mean_matrix(width, group):
    i = lax.broadcasted_iota(jnp.int32, (width, width), 0)
    j = lax.broadcasted_iota(jnp.int32, (width, width), 1)
    return jnp.where((i // group) == (j // group), 1.0 / group, 0.0).astype(bf16)


def _rope(x, cos, sin):
    half = ROT_DIM // 2
    outs = []
    for c in range(x.shape[1] // LANES):
        xc = x[:, c * LANES:(c + 1) * LANES]
        lane = lax.broadcasted_iota(jnp.int32, xc.shape, 1)
        partner = jnp.where((lane % HEAD_DIM) < half,
                            pltpu.roll(xc, LANES - half, axis=1),
                            pltpu.roll(xc, half, axis=1))
        outs.append(xc * cos[:, c * LANES:(c + 1) * LANES] + partner * sin[:, c * LANES:(c + 1) * LANES])
    return outs[0] if len(outs) == 1 else jnp.concatenate(outs, axis=1)


def _softplus(z):
    return jnp.maximum(z, 0.0) + jnp.log1p(jnp.exp(-jnp.abs(z)))


def _sigmoid(z):
    return 1.0 / (1.0 + jnp.exp(-z))


def _inproj_kernel(x_ref, g_ref, w_ref, o_ref, h_ref):
    @pl.when(pl.program_id(1) == 0)
    def _():
        x = x_ref[...]
        ms = jnp.mean(x * x, axis=-1, keepdims=True)
        h_ref[...] = (x * lax.rsqrt(ms + EPS) * g_ref[...]).astype(bf16)

    o_ref[...] = _dot(h_ref[...], w_ref[...])


def _inproj(x2d, gain, w):
    m, d = x2d.shape
    n = w.shape[1]
    tm, tn = 1024, 1024
    return pl.pallas_call(
        _inproj_kernel,
        grid=(m // tm, n // tn),
        in_specs=[pl.BlockSpec((tm, d), lambda i, j: (i, 0)),
                  pl.BlockSpec((1, d), lambda i, j: (0, 0)),
                  pl.BlockSpec((d, tn), lambda i, j: (0, j))],
        out_specs=pl.BlockSpec((tm, tn), lambda i, j: (i, j)),
        out_shape=jax.ShapeDtypeStruct((m, n), f32),
        scratch_shapes=[pltpu.VMEM((tm, d), bf16)],
        compiler_params=pltpu.CompilerParams(
            dimension_semantics=("parallel", "arbitrary"), vmem_limit_bytes=48 * 2 ** 20),
        name="inproj",
    )(x2d, gain, w)


def _sb_kernel(q_ref, k_ref, v_ref, o_ref, kb_ref, vb_ref, qh_ref, c_ref, acc_ref):
    qi = pl.program_id(1)

    @pl.when(qi == 0)
    def _():
        kb_ref[...] = k_ref[0].astype(bf16)
        vb_ref[...] = v_ref[0].astype(bf16)

    lane = lax.broadcasted_iota(jnp.int32, (QB, LANES), 1)
    row = lax.broadcasted_iota(jnp.int32, (QB, LANES), 0)
    jj = lax.broadcasted_iota(jnp.int32, (LANES, 2 * LANES), 0)
    ss = lax.broadcasted_iota(jnp.int32, (LANES, 2 * LANES), 1)
    later_and_sum = jnp.where((ss >= LANES) | (jj > ss), 1.0, 0.0).astype(bf16)

    q = q_ref[0] * (ATT_SCALE * LOG2_E)
    for h in range(N_SB):
        qp = q[:, (h // 2) * LANES:(h // 2 + 1) * LANES]
        qh_ref[h] = jnp.where((lane // HEAD_DIM) == (h % 2), qp, 0.0).astype(bf16)
    c_ref[...] = jnp.zeros_like(c_ref)
    acc_ref[...] = jnp.zeros_like(acc_ref)

    def block(kb, diagonal):
        off = pl.multiple_of(kb * QB, QB)
        past = lane < row
        kblk = kb_ref[pl.ds(off, QB), :]
        vblk = vb_ref[pl.ds(off, QB), :]
        c_old = [c_ref[h] for h in range(N_SB)]
        acc_old = [acc_ref[h] for h in range(N_SB)]
        heads = range(N_SB)
        cols = [slice((h // 2) * LANES, (h // 2 + 1) * LANES) for h in heads]
        z = [_dot_nt(qh_ref[h], kblk[:, cols[h]]) for h in heads]
        log_1m = [jnp.minimum(-z[h], 0.0) - jnp.log2(1.0 + jnp.exp2(-jnp.abs(z[h]))) for h in heads]
        lm = [jnp.where(past, log_1m[h], 0.0) if diagonal else log_1m[h] for h in heads]
        r = [_dot2_right(lm[h], later_and_sum) for h in heads]
        log_a = [z[h] + log_1m[h] + r[h][:, :LANES] + c_old[h] for h in heads]
        if diagonal:
            log_a = [jnp.where(past, log_a[h], NEG) for h in heads]
        a = [jnp.exp2(log_a[h]).astype(bf16) for h in heads]
        acc_new = [acc_old[h] + _dot(a[h], vblk[:, cols[h]]) for h in heads]
        c_new = [c_old[h] + r[h][:, LANES:] for h in heads]
        cmax = c_new[0]
        for h in range(N_SB):
            acc_ref[h] = acc_new[h]
            c_ref[h] = c_new[h]
            cmax = jnp.maximum(cmax, c_new[h])
        return jnp.max(cmax)

    def cond(state):
        kb, cmax = state
        return (kb >= 0) & (cmax >= F32_EXP2_UNDERFLOW)

    def body(state):
        kb, _ = state
        return kb - 1, block(kb, False)

    lax.while_loop(cond, body, (qi - 1, block(qi, True)))
    for p in range(N_SB // 2):
        o_ref[0, :, p * LANES:(p + 1) * LANES] = jnp.where(
            lane < HEAD_DIM, acc_ref[2 * p], acc_ref[2 * p + 1]).astype(o_ref.dtype)


def _sb_attention(p3):
    b, t, _ = p3.shape
    return pl.pallas_call(
        _sb_kernel,
        grid=(b, t // QB),
        in_specs=[pl.BlockSpec((1, QB, W_SB), lambda bi, qi: (bi, qi, COL_SQ // W_SB)),
                  pl.BlockSpec((1, t, W_SB), lambda bi, qi: (bi, 0, COL_SK // W_SB)),
                  pl.BlockSpec((1, t, W_SB), lambda bi, qi: (bi, 0, COL_SV // W_SB))],
        out_specs=pl.BlockSpec((1, QB, W_SB), lambda bi, qi: (bi, qi, 0)),
        out_shape=jax.ShapeDtypeStruct((b, t, W_SB), bf16),
        scratch_shapes=[pltpu.VMEM((t, W_SB), bf16), pltpu.VMEM((t, W_SB), bf16),
                        pltpu.VMEM((N_SB, QB, LANES), bf16),
                        pltpu.VMEM((N_SB, QB, LANES), f32), pltpu.VMEM((N_SB, QB, LANES), f32)],
        compiler_params=pltpu.CompilerParams(
            dimension_semantics=("parallel", "arbitrary"), vmem_limit_bytes=48 * 2 ** 20),
        name="sb_attention",
    )(p3, p3, p3)


def _float_key(x):
    bits = pltpu.bitcast(x, jnp.int32)
    return jnp.where(bits < 0, bits ^ jnp.int32(0x7FFFFFFF), bits)


def _np_float_key(v):
    bits = int(np.array(v, np.float32).view(np.int32))
    return bits ^ 0x7FFFFFFF if bits < 0 else bits


_KEY_HALF_NEG = _np_float_key(0.5 * NEG)
_INT_MIN = -2 ** 31


def _dsa_kernel(dq_ref, kv_ref, iq_ref, ik_ref, iw_ref, cq_ref, sq_ref, ck_ref, sk_ref,
                qn_ref, kn_ref, o_ref, kn2_ref, v2_ref, ki2_ref, key_ref, bias_ref, *, k_top):
    qi = pl.program_id(1)
    t = kv_ref.shape[1]

    @pl.when(qi == 0)
    def _():
        lane = lax.broadcasted_iota(jnp.int32, (t, LANES), 1)
        kv = kv_ref[0]
        swapped = pltpu.roll(kv, HEAD_DIM, axis=1)
        k2 = jnp.where(lane < HEAD_DIM, kv, swapped)
        v2_ref[...] = jnp.where(lane < HEAD_DIM, swapped, kv).astype(bf16)
        ms = jnp.mean(k2 * k2, axis=-1, keepdims=True)
        kn2 = k2 * lax.rsqrt(ms + EPS) * kn_ref[...]
        kn2_ref[...] = _rope(kn2, ck_ref[...], sk_ref[...]).astype(bf16)
        ik = ik_ref[0]
        ik2 = jnp.where(lane < HEAD_DIM, ik, pltpu.roll(ik, HEAD_DIM, axis=1))
        ki2_ref[...] = _rope(ik2, ck_ref[...], sk_ref[...]).astype(bf16)

    cos = cq_ref[...]
    sin = sq_ref[...]
    lane = lax.broadcasted_iota(jnp.int32, (QB, LANES), 1)
    half_id = lane // HEAD_DIM

    iq = _rope(iq_ref[0], cos, sin)
    iw = iw_ref[0] * IDX_SCALE
    score = jnp.zeros((QB, t), f32)
    for j in range(N_IDX):
        pair = iq[:, (j // 2) * LANES:(j // 2 + 1) * LANES]
        qj = jnp.where(half_id == (j % 2), pair, 0.0).astype(bf16)
        rel = jnp.maximum(_dot_nt(qj, ki2_ref[...]), 0.0)
        score = score + iw[:, j:j + 1] * rel
    col = lax.broadcasted_iota(jnp.int32, (QB, t), 1)
    row = lax.broadcasted_iota(jnp.int32, (QB, t), 0) + qi * QB
    score = jnp.where(col <= row, score + 0.0, NEG)
    key_ref[...] = _float_key(score)

    kf = float(k_top)

    def count_ge(trial):
        return jnp.sum(jnp.where(key_ref[...] >= trial, 1.0, 0.0), axis=-1, keepdims=True)

    lo = jnp.where(count_ge(jnp.zeros((QB, 1), jnp.int32)) >= kf,
                   jnp.int32(0), jnp.int32(_INT_MIN))

    def bit_step(i, lo):
        trial = lo | jnp.left_shift(jnp.int32(1), 30 - i)
        return jnp.where(count_ge(trial) >= kf, trial, lo)

    thr = lax.fori_loop(0, 31, bit_step, lo)

    keyv = key_ref[...]
    adm = keyv > _KEY_HALF_NEG
    cnt_gt = jnp.sum(jnp.where(keyv > thr, 1.0, 0.0), axis=-1, keepdims=True)
    n_eq = jnp.sum(jnp.where(keyv == thr, 1.0, 0.0), axis=-1, keepdims=True)
    need = kf - cnt_gt
    bias_ref[...] = jnp.where(adm & (keyv >= thr), 0.0, NEG)

    tie_overflow = jnp.where((n_eq > need) & (thr > _KEY_HALF_NEG), 1.0, 0.0)

    @pl.when(jnp.max(tie_overflow) > 0.0)
    def _():
        i = lax.broadcasted_iota(jnp.int32, (LANES, LANES), 0)
        j = lax.broadcasted_iota(jnp.int32, (LANES, LANES), 1)
        before = jnp.where(i < j, 1.0, 0.0).astype(bf16)
        seen = jnp.zeros((QB, 1), f32)
        for cb in range(t // LANES):
            kb = key_ref[:, cb * LANES:(cb + 1) * LANES]
            eq = jnp.where(kb == thr, 1.0, 0.0)
            rank = _dot(eq.astype(bf16), before) + seen
            seen = seen + jnp.sum(eq, axis=-1, keepdims=True)
            sel = (kb > _KEY_HALF_NEG) & ((kb > thr) | ((kb == thr) & (rank < need)))
            bias_ref[:, cb * LANES:(cb + 1) * LANES] = jnp.where(sel, 0.0, NEG)

    x = dq_ref[0]
    ms = _dot3_right(x * x, _group_mean_matrix(W_DSA, HEAD_DIM))
    qn = x * lax.rsqrt(ms + EPS) * qn_ref[...]
    qn = _rope(qn, cos[:, :W_DSA], sin[:, :W_DSA]) * ATT_SCALE
    bias = bias_ref[...]
    for pair in range(N_DSA // 2):
        qp = qn[:, pair * LANES:(pair + 1) * LANES]
        res = []
        for hh in range(2):
            qh = jnp.where(half_id == hh, qp, 0.0).astype(bf16)
            logits = _dot_nt(qh, kn2_ref[...]) + bias
            m = jnp.max(logits, axis=-1, keepdims=True)
            p = jnp.exp(logits - m)
            l = jnp.sum(p, axis=-1, keepdims=True)
            res.append(_dot(p.astype(bf16), v2_ref[...]) / l)
        o_ref[0, :, pair * LANES:(pair + 1) * LANES] = jnp.where(
            lane < HEAD_DIM, res[0], res[1]).astype(o_ref.dtype)


def _dsa_attention(p3, cos_t, sin_t, qn_gain, kn_gain):
    b, t, _ = p3.shape
    k_top = min(TOPK_MAX, t // 4)
    return pl.pallas_call(
        functools.partial(_dsa_kernel, k_top=k_top),
        grid=(b, t // QB),
        in_specs=[
            pl.BlockSpec((1, QB, W_DSA), lambda bi, qi: (bi, qi, COL_DQ // W_DSA)),
            pl.BlockSpec((1, t, LANES), lambda bi, qi: (bi, 0, COL_DKV // LANES)),
            pl.BlockSpec((1, QB, N_IDX * IDX_DIM), lambda bi, qi: (bi, qi, COL_IQ // (N_IDX * IDX_DIM))),
            pl.BlockSpec((1, t, LANES), lambda bi, qi: (bi, 0, COL_IK // LANES)),
            pl.BlockSpec((1, QB, LANES), lambda bi, qi: (bi, qi, COL_IW // LANES)),
            pl.BlockSpec((QB, 4 * LANES), lambda bi, qi: (qi, 0)),
            pl.BlockSpec((QB, 4 * LANES), lambda bi, qi: (qi, 0)),
            pl.BlockSpec((t, LANES), lambda bi, qi: (0, 0)),
            pl.BlockSpec((t, LANES), lambda bi, qi: (0, 0)),
            pl.BlockSpec((1, W_DSA), lambda bi, qi: (0, 0)),
            pl.BlockSpec((1, LANES), lambda bi, qi: (0, 0)),
        ],
        out_specs=pl.BlockSpec((1, QB, W_DSA), lambda bi, qi: (bi, qi, 0)),
        out_shape=jax.ShapeDtypeStruct((b, t, W_DSA), bf16),
        scratch_shapes=[pltpu.VMEM((t, LANES), bf16), pltpu.VMEM((t, LANES), bf16),
                        pltpu.VMEM((t, LANES), bf16), pltpu.VMEM((QB, t), jnp.int32),
                        pltpu.VMEM((QB, t), f32)],
        compiler_params=pltpu.CompilerParams(
            dimension_semantics=("parallel", "arbitrary"), vmem_limit_bytes=48 * 2 ** 20),
        name="dsa_attention",
    )(p3, p3, p3, p3, p3, cos_t, sin_t, cos_t, sin_t, qn_gain, kn_gain)


def _hgrn_kernel(hq_ref, hf_ref, hi_ref, vt_ref, lb_ref, o_ref, st_ref):
    @pl.when(pl.program_id(1) == 0)
    def _():
        st_ref[...] = jnp.zeros_like(st_ref)

    c = HG_CHUNK
    nsub = c // HG_SUB
    ti = lax.broadcasted_iota(jnp.int32, (c, c), 0)
    si = lax.broadcasted_iota(jnp.int32, (c, c), 1)
    cum = jnp.where(si <= ti, 1.0, 0.0).astype(bf16)
    same_sub = (ti // HG_SUB) == (si // HG_SUB)
    later_sub = (ti // HG_SUB) > (si // HG_SUB)
    delta = ti - si
    rsub = lax.broadcasted_iota(jnp.int32, (c, HG_DK), 0) // HG_SUB

    for sub in range(HG_ROWS // c):
        r0 = sub * c
        for h in range(N_HG):
            qraw = hq_ref[0, r0:r0 + c, h * HG_DK:(h + 1) * HG_DK]
            fpre = hf_ref[0, r0:r0 + c, h * HG_DK:(h + 1) * HG_DK]
            lb = lb_ref[:, h * HG_DK:(h + 1) * HG_DK]
            v = hi_ref[0, r0:r0 + c, h * HG_DV:(h + 1) * HG_DV]
            vt = vt_ref[0, h, sub]
            q = qraw * _sigmoid(qraw)
            f = lb + (1.0 - lb) * _sigmoid(fpre)
            g = jnp.log(jnp.maximum(f, F_MIN))
            kk = 1.0 - f
            b = _dot3_left(cum, g)
            b_last = b[c - 1:c, :]
            ends = [b[(j + 1) * HG_SUB - 1:(j + 1) * HG_SUB, :] for j in range(nsub)]
            b_end = jnp.concatenate([jnp.broadcast_to(e, (HG_SUB, HG_DK)) for e in ends], axis=0)

            k_rel = kk * jnp.exp(b_end - b)
            k_cat = jnp.concatenate([jnp.where(rsub == j, k_rel, 0.0) for j in range(nsub - 1)], axis=1)
            q_cat = jnp.concatenate([q * jnp.exp(jnp.minimum(b - ends[j], 0.0)) for j in range(nsub - 1)],
                                    axis=1)
            att = jnp.where(later_sub, _dot_nt(q_cat.astype(bf16), k_cat.astype(bf16)), 0.0)

            for d in range(HG_SUB):
                if d == 0:
                    term = q * kk
                else:
                    kd = pltpu.roll(kk, d, axis=0)
                    bd = pltpu.roll(b, d, axis=0)
                    term = q * kd * jnp.exp(jnp.minimum(b - bd, 0.0))
                s = jnp.sum(term, axis=-1, keepdims=True)
                att = att + jnp.where(same_sub & (delta == d), s, 0.0)

            st = st_ref[h]
            o = _dot(att.astype(bf16), v.astype(bf16)) + _dot_nt((q * jnp.exp(b)).astype(bf16), st.astype(bf16))
            o_ref[0, h, r0:r0 + c, :] = o
            k_hat = kk * jnp.exp(b_last - b)
            st_ref[h] = st * jnp.exp(b_last) + _dot(vt.astype(bf16), k_hat.astype(bf16))


def _hgrn(p3, vt, lb):
    b, t, _ = p3.shape
    per = HG_ROWS // HG_CHUNK
    return pl.pallas_call(
        _hgrn_kernel,
        grid=(b, t // HG_ROWS),
        in_specs=[pl.BlockSpec((1, HG_ROWS, W_HF), lambda bi, i: (bi, i, COL_HQ // W_HF)),
                  pl.BlockSpec((1, HG_ROWS, W_HF), lambda bi, i: (bi, i, COL_HF // W_HF)),
                  pl.BlockSpec((1, HG_ROWS, W_HV), lambda bi, i: (bi, i, COL_HI // W_HV)),
                  pl.BlockSpec((1, N_HG, per, HG_DV, HG_CHUNK), lambda bi, i: (bi, 0, i, 0, 0)),
                  pl.BlockSpec((1, W_HF), lambda bi, i: (0, 0))],
        out_specs=pl.BlockSpec((1, N_HG, HG_ROWS, HG_DV), lambda bi, i: (bi, 0, i, 0)),
        out_shape=jax.ShapeDtypeStruct((b, N_HG, t, HG_DV), f32),
        scratch_shapes=[pltpu.VMEM((N_HG, HG_DV, HG_DK), f32)],
        compiler_params=pltpu.CompilerParams(dimension_semantics=("parallel", "arbitrary")),
        name="hgrn2",
    )(p3, p3, p3, vt, lb)


def _merge_kernel(x_ref, sb_ref, dsa_ref, hg_ref, gs_ref, gd_ref, gh_ref, og_ref, on_ref,
                  wsb_ref, wdsa_ref, whg_ref, wout_ref, o_ref):
    o = hg_ref[...]
    ms = _dot3_right(o * o, _group_mean_matrix(W_HV, HG_DV))
    gate = og_ref[...]
    og = o * lax.rsqrt(ms + EPS) * on_ref[...] * (gate * _sigmoid(gate))
    y_hg = _dot(og.astype(bf16), whg_ref[...])
    y_sb = _dot(sb_ref[...], wsb_ref[...])
    y_dsa = _dot(dsa_ref[...], wdsa_ref[...])
    mixed = (_sigmoid(gs_ref[...]) * y_sb + _sigmoid(gd_ref[...]) * y_dsa + _sigmoid(gh_ref[...]) * y_hg)
    o_ref[...] = x_ref[...] + _dot(mixed.astype(bf16), wout_ref[...])


def _merge(x2d, att_sb, att_dsa, o_hg, p2d, onorm, w_sb, w_dsa, w_hg, w_out):
    m, d = x2d.shape
    tm = 512
    row = lambda w: pl.BlockSpec((tm, w), lambda i: (i, 0))
    full = lambda a: pl.BlockSpec(a.shape, lambda i: (0, 0))
    return pl.pallas_call(
        _merge_kernel,
        grid=(m // tm,),
        in_specs=[row(d), row(W_SB), row(W_DSA), row(W_HV),
                  pl.BlockSpec((tm, d), lambda i: (i, COL_G // d)),
                  pl.BlockSpec((tm, d), lambda i: (i, COL_G // d + 1)),
                  pl.BlockSpec((tm, d), lambda i: (i, COL_G // d + 2)),
                  pl.BlockSpec((tm, W_HV), lambda i: (i, COL_HG // W_HV)),
                  full(onorm), full(w_sb), full(w_dsa), full(w_hg), full(w_out)],
        out_specs=row(d),
        out_shape=jax.ShapeDtypeStruct((m, d), f32),
        compiler_params=pltpu.CompilerParams(
            dimension_semantics=("parallel",), vmem_limit_bytes=48 * 2 ** 20),
        name="merge_outproj",
    )(x2d, att_sb, att_dsa, o_hg, p2d, p2d, p2d, p2d, onorm, w_sb, w_dsa, w_hg, w_out)


def _mlp_kernel(x_ref, g_ref, wu_ref, wd_ref, o_ref, h_ref, acc_ref):
    j = pl.program_id(1)

    @pl.when(j == 0)
    def _():
        x = x_ref[...]
        ms = jnp.mean(x * x, axis=-1, keepdims=True)
        h_ref[...] = (x * lax.rsqrt(ms + EPS) * g_ref[...]).astype(bf16)
        acc_ref[...] = jnp.zeros_like(acc_ref)

    u = jnp.maximum(_dot(h_ref[...], wu_ref[...]), 0.0)
    acc_ref[...] += _dot((u * u).astype(bf16), wd_ref[...])

    @pl.when(j == pl.num_programs(1) - 1)
    def _():
        o_ref[...] = x_ref[...] + acc_ref[...]


def _mlp(x2d, gain, w_up, w_down):
    m, d = x2d.shape
    dff = w_up.shape[1]
    tm, tf = 1024, 512
    return pl.pallas_call(
        _mlp_kernel,
        grid=(m // tm, dff // tf),
        in_specs=[pl.BlockSpec((tm, d), lambda i, j: (i, 0)),
                  pl.BlockSpec((1, d), lambda i, j: (0, 0)),
                  pl.BlockSpec((d, tf), lambda i, j: (0, j)),
                  pl.BlockSpec((tf, d), lambda i, j: (j, 0))],
        out_specs=pl.BlockSpec((tm, d), lambda i, j: (i, 0)),
        out_shape=jax.ShapeDtypeStruct((m, d), f32),
        scratch_shapes=[pltpu.VMEM((tm, d), bf16), pltpu.VMEM((tm, d), f32)],
        compiler_params=pltpu.CompilerParams(
            dimension_semantics=("parallel", "arbitrary"), vmem_limit_bytes=48 * 2 ** 20),
        name="mlp",
    )(x2d, gain, w_up, w_down)


def _reorder_w_in(w):
    widths = (W_SB, W_SB, W_SB, W_DSA, HEAD_DIM, HEAD_DIM, N_IDX * IDX_DIM, IDX_DIM, N_IDX,
              W_HF, W_HF, W_HV, W_HV, w.shape[0], w.shape[0], w.shape[0])
    pts = np.concatenate([[0], np.cumsum(widths)])
    seg = {n: w[:, int(pts[i]):int(pts[i + 1])] for i, n in enumerate(_IN_NAMES)}
    zeros = lambda n: jnp.zeros((w.shape[0], n), w.dtype)
    cols = [seg["g_sb"], seg["g_dsa"], seg["g_hg"], seg["hq"], seg["hf"], seg["iq"], seg["hi"], seg["hg"],
            seg["dk"], seg["dv"], seg["ik"], zeros(LANES - IDX_DIM),
            seg["sq"], seg["sk"], seg["sv"], seg["dq"], seg["iw"], zeros(LANES - N_IDX),
            zeros(NP - COL_IW - LANES)]
    out = jnp.concatenate(cols, axis=1).astype(bf16)
    assert out.shape[1] == NP
    return out


def _rope_tables(t):
    half = ROT_DIM // 2
    inv = ROPE_THETA ** (-(np.arange(half, dtype=np.float32) * 2.0) / ROT_DIM)
    ang = jnp.arange(t, dtype=f32)[:, None] * jnp.asarray(inv, f32)[None, :]
    cos, sin = jnp.cos(ang), jnp.sin(ang)
    pad = HEAD_DIM - ROT_DIM
    cos_h = jnp.concatenate([cos, cos, jnp.ones((t, pad), f32)], axis=1)
    sin_h = jnp.concatenate([-sin, sin, jnp.zeros((t, pad), f32)], axis=1)
    reps = 4 * LANES // HEAD_DIM
    return jnp.tile(cos_h, (1, reps)), jnp.tile(sin_h, (1, reps))


def kernel(x, norm_mix, w_in, qn_dsa, kn_dsa, hgrn_lb, hgrn_onorm, w_br_sb, w_br_dsa, w_br_hgrn,
           w_out, norm_mlp, w_up, w_down):
    bsz, t, d = x.shape
    depth = w_in.shape[0]
    m = bsz * t
    assert d == 1024 and t % QB == 0 and m % 1024 == 0

    p_lb = jax.nn.softmax(hgrn_lb.astype(f32), axis=0)
    lbs = jnp.cumsum(p_lb, axis=0) - p_lb[0:1]
    cos_t, sin_t = _rope_tables(t)

    xf = x.reshape(m, d)
    for l in range(depth):
        p2d = _inproj(xf, norm_mix[l][None, :], _reorder_w_in(w_in[l]))
        p3 = p2d.reshape(bsz, t, NP)
        att_sb = _sb_attention(p3)
        att_dsa = _dsa_attention(p3, cos_t, sin_t, jnp.tile(qn_dsa[l], N_DSA)[None, :],
                                 jnp.tile(kn_dsa[l], 2)[None, :])
        hi = p3[:, :, COL_HI:COL_HI + W_HV]
        vt = jnp.transpose(hi.reshape(bsz, t // HG_CHUNK, HG_CHUNK, N_HG, HG_DV), (0, 3, 1, 4, 2))
        o_hg = _hgrn(p3, vt, lbs[l][None, :])
        o_hg = jnp.transpose(o_hg, (0, 2, 1, 3)).reshape(m, W_HV)
        xf = _merge(xf, att_sb.reshape(m, W_SB), att_dsa.reshape(m, W_DSA), o_hg, p2d,
                    jnp.tile(hgrn_onorm[l], N_HG)[None, :],
                    w_br_sb[l].astype(bf16), w_br_dsa[l].astype(bf16), w_br_hgrn[l].astype(bf16),
                    w_out[l].astype(bf16))
        xf = _mlp(xf, norm_mlp[l][None, :], w_up[l].astype(bf16), w_down[l].astype(bf16))
    return xf.reshape(bsz, t, d)
```

```python
import functools

import numpy as np
import jax
import jax.numpy as jnp
from jax import lax
from jax.experimental import pallas as pl
from jax.experimental.pallas import tpu as pltpu

f32 = jnp.float32
bf16 = jnp.bfloat16

HEAD_DIM = 64
N_SB = 6
N_DSA = 6
N_IDX = 8
IDX_DIM = 64
TOPK_MAX = 256
N_HG = 4
HG_DK = 128
HG_DV = 64
ROPE_THETA = 500000.0
ROT_DIM = HEAD_DIM // 4
EPS = 1e-6
F_MIN = 1e-12
NEG = -1e30
W_SB = N_SB * HEAD_DIM
W_DSA = N_DSA * HEAD_DIM
W_HF = N_HG * HG_DK
W_HV = N_HG * HG_DV
IDX_SCALE = (IDX_DIM * N_IDX) ** -0.5
ATT_SCALE = HEAD_DIM ** -0.5
LOG2_E = 1.4426950408889634
F32_EXP2_UNDERFLOW = -150.0

_IN_NAMES = ("sq", "sk", "sv", "dq", "dk", "dv", "iq", "ik", "iw",
             "hq", "hf", "hi", "hg", "g_sb", "g_dsa", "g_hg")

LANES = 128
QB = 128
DSA_CH = 512
DSA_SEL = 256
SOFTMAX_MAX_FLOOR = -1e29
HG_CHUNK = 64
HG_SUB = 16
HG_ROWS = 128

COL_G = 0
COL_HQ = 3072
COL_HF = 3584
COL_IQ = 4096
COL_HI = 4608
COL_HG = 4864
COL_DKV = 5120
COL_IK = 5248
COL_SQ = 5376
COL_SK = 5760
COL_SV = 6144
COL_DQ = 6528
COL_IW = 6912
NP = 7168


def _dot(a, b):
    return jnp.dot(a, b, preferred_element_type=f32)


def _dot_nt(a, b):
    return lax.dot_general(a, b, (((1,), (1,)), ((), ())), preferred_element_type=f32)


def _split3(x):
    hi = x.astype(bf16)
    r = x - hi.astype(f32)
    mid = r.astype(bf16)
    lo = (r - mid.astype(f32)).astype(bf16)
    return hi, mid, lo


def _dot3_right(x, m):
    hi, mid, lo = _split3(x)
    return _dot(hi, m) + _dot(mid, m) + _dot(lo, m)


def _dot2_right(x, m):
    hi = x.astype(bf16)
    lo = (x - hi.astype(f32)).astype(bf16)
    return _dot(hi, m) + _dot(lo, m)


def _dot3_left(m, x):
    hi, mid, lo = _split3(x)
    return _dot(m, hi) + _dot(m, mid) + _dot(m, lo)


def _group_mean_matrix(width, group):
    i = lax.broadcasted_iota(jnp.int32, (width, width), 0)
    j = lax.broadcasted_iota(jnp.int32, (width, width), 1)
    return jnp.where((i // group) == (j // group), 1.0 / group, 0.0).astype(bf16)


def _rope(x, cos, sin):
    half = ROT_DIM // 2
    outs = []
    for c in range(x.shape[1] // LANES):
        xc = x[:, c * LANES:(c + 1) * LANES]
        lane = lax.broadcasted_iota(jnp.int32, xc.shape, 1)
        partner = jnp.where((lane % HEAD_DIM) < half,
                            pltpu.roll(xc, LANES - half, axis=1),
                            pltpu.roll(xc, half, axis=1))
        outs.append(xc * cos[:, c * LANES:(c + 1) * LANES] + partner * sin[:, c * LANES:(c + 1) * LANES])
    return outs[0] if len(outs) == 1 else jnp.concatenate(outs, axis=1)


def _softplus(z):
    return jnp.maximum(z, 0.0) + jnp.log1p(jnp.exp(-jnp.abs(z)))


def _sigmoid(z):
    return 1.0 / (1.0 + jnp.exp(-z))


def _inproj_kernel(x_ref, g_ref, w_ref, o_ref, h_ref):
    @pl.when(pl.program_id(1) == 0)
    def _():
        x = x_ref[...]
        ms = jnp.mean(x * x, axis=-1, keepdims=True)
        h_ref[...] = (x * lax.rsqrt(ms + EPS) * g_ref[...]).astype(bf16)

    o_ref[...] = _dot(h_ref[...], w_ref[...])


def _inproj(x2d, gain, w):
    m, d = x2d.shape
    n = w.shape[1]
    tm, tn = 1024, 1024
    return pl.pallas_call(
        _inproj_kernel,
        grid=(m // tm, n // tn),
        in_specs=[pl.BlockSpec((tm, d), lambda i, j: (i, 0)),
                  pl.BlockSpec((1, d), lambda i, j: (0, 0)),
                  pl.BlockSpec((d, tn), lambda i, j: (0, j))],
        out_specs=pl.BlockSpec((tm, tn), lambda i, j: (i, j)),
        out_shape=jax.ShapeDtypeStruct((m, n), f32),
        scratch_shapes=[pltpu.VMEM((tm, d), bf16)],
        compiler_params=pltpu.CompilerParams(
            dimension_semantics=("parallel", "arbitrary"), vmem_limit_bytes=48 * 2 ** 20),
        name="inproj",
    )(x2d, gain, w)


def _sb_kernel(q_ref, k_ref, v_ref, o_ref, kb_ref, vb_ref, qh_ref, c_ref, acc_ref):
    qi = pl.program_id(1)

    @pl.when(qi == 0)
    def _():
        kb_ref[...] = k_ref[0].astype(bf16)
        vb_ref[...] = v_ref[0].astype(bf16)

    lane = lax.broadcasted_iota(jnp.int32, (QB, LANES), 1)
    row = lax.broadcasted_iota(jnp.int32, (QB, LANES), 0)
    jj = lax.broadcasted_iota(jnp.int32, (LANES, 2 * LANES), 0)
    ss = lax.broadcasted_iota(jnp.int32, (LANES, 2 * LANES), 1)
    later_and_sum = jnp.where((ss >= LANES) | (jj > ss), 1.0, 0.0).astype(bf16)

    q = q_ref[0] * (ATT_SCALE * LOG2_E)
    for h in range(N_SB):
        qp = q[:, (h // 2) * LANES:(h // 2 + 1) * LANES]
        qh_ref[h] = jnp.where((lane // HEAD_DIM) == (h % 2), qp, 0.0).astype(bf16)
    c_ref[...] = jnp.zeros_like(c_ref)
    acc_ref[...] = jnp.zeros_like(acc_ref)

    def block(kb, diagonal):
        off = pl.multiple_of(kb * QB, QB)
        past = lane < row
        kblk = kb_ref[pl.ds(off, QB), :]
        vblk = vb_ref[pl.ds(off, QB), :]
        c_old = [c_ref[h] for h in range(N_SB)]
        acc_old = [acc_ref[h] for h in range(N_SB)]
        heads = range(N_SB)
        cols = [slice((h // 2) * LANES, (h // 2 + 1) * LANES) for h in heads]
        z = [_dot_nt(qh_ref[h], kblk[:, cols[h]]) for h in heads]
        log_1m = [jnp.minimum(-z[h], 0.0) - jnp.log2(1.0 + jnp.exp2(-jnp.abs(z[h]))) for h in heads]
        lm = [jnp.where(past, log_1m[h], 0.0) if diagonal else log_1m[h] for h in heads]
        r = [_dot2_right(lm[h], later_and_sum) for h in heads]
        log_a = [z[h] + log_1m[h] + r[h][:, :LANES] + c_old[h] for h in heads]
        if diagonal:
            log_a = [jnp.where(past, log_a[h], NEG) for h in heads]
        a = [jnp.exp2(log_a[h]).astype(bf16) for h in heads]
        acc_new = [acc_old[h] + _dot(a[h], vblk[:, cols[h]]) for h in heads]
        c_new = [c_old[h] + r[h][:, LANES:] for h in heads]
        cmax = c_new[0]
        for h in range(N_SB):
            acc_ref[h] = acc_new[h]
            c_ref[h] = c_new[h]
            cmax = jnp.maximum(cmax, c_new[h])
        return jnp.max(cmax)

    def cond(state):
        kb, cmax = state
        return (kb >= 0) & (cmax >= F32_EXP2_UNDERFLOW)

    def body(state):
        kb, _ = state
        return kb - 1, block(kb, False)

    lax.while_loop(cond, body, (qi - 1, block(qi, True)))
    for p in range(N_SB // 2):
        o_ref[0, :, p * LANES:(p + 1) * LANES] = jnp.where(
            lane < HEAD_DIM, acc_ref[2 * p], acc_ref[2 * p + 1]).astype(o_ref.dtype)


def _sb_attention(p3):
    b, t, _ = p3.shape
    return pl.pallas_call(
        _sb_kernel,
        grid=(b, t // QB),
        in_specs=[pl.BlockSpec((1, QB, W_SB), lambda bi, qi: (bi, qi, COL_SQ // W_SB)),
                  pl.BlockSpec((1, t, W_SB), lambda bi, qi: (bi, 0, COL_SK // W_SB)),
                  pl.BlockSpec((1, t, W_SB), lambda bi, qi: (bi, 0, COL_SV // W_SB))],
        out_specs=pl.BlockSpec((1, QB, W_SB), lambda bi, qi: (bi, qi, 0)),
        out_shape=jax.ShapeDtypeStruct((b, t, W_SB), bf16),
        scratch_shapes=[pltpu.VMEM((t, W_SB), bf16), pltpu.VMEM((t, W_SB), bf16),
                        pltpu.VMEM((N_SB, QB, LANES), bf16),
                        pltpu.VMEM((N_SB, QB, LANES), f32), pltpu.VMEM((N_SB, QB, LANES), f32)],
        compiler_params=pltpu.CompilerParams(
            dimension_semantics=("parallel", "arbitrary"), vmem_limit_bytes=48 * 2 ** 20),
        name="sb_attention",
    )(p3, p3, p3)


def _float_key(x):
    bits = pltpu.bitcast(x, jnp.int32)
    return jnp.where(bits < 0, bits ^ jnp.int32(0x7FFFFFFF), bits)


def _np_float_key(v):
    bits = int(np.array(v, np.float32).view(np.int32))
    return bits ^ 0x7FFFFFFF if bits < 0 else bits


_KEY_HALF_NEG = _np_float_key(0.5 * NEG)
_INT_MIN = -2 ** 31


def _dsa_kernel(dq_ref, kv_ref, iq_ref, ik_ref, iw_ref, cq_ref, sq_ref, ck_ref, sk_ref,
                qn_ref, kn_ref, o_ref, kn2_ref, vt_ref, ki2_ref, key_ref, bias_ref, *, k_top):
    qi = pl.program_id(1)
    t = kv_ref.shape[1]
    ch = DSA_CH
    nch = (qi * QB + QB + ch - 1) // ch

    @pl.when(qi == 0)
    def _():
        lane = lax.broadcasted_iota(jnp.int32, (t, LANES), 1)
        kv = kv_ref[0]
        k2 = jnp.where(lane < HEAD_DIM, kv, pltpu.roll(kv, HEAD_DIM, axis=1))
        ms = jnp.mean(k2 * k2, axis=-1, keepdims=True)
        kn2 = k2 * lax.rsqrt(ms + EPS) * kn_ref[...]
        kn2_ref[...] = _rope(kn2, ck_ref[...], sk_ref[...]).astype(bf16)
        ik = ik_ref[0]
        ik2 = jnp.where(lane < HEAD_DIM, ik, pltpu.roll(ik, HEAD_DIM, axis=1))
        ki2_ref[...] = _rope(ik2, ck_ref[...], sk_ref[...]).astype(bf16)
        for c in range(t // ch):
            vt_ref[c] = kv[c * ch:(c + 1) * ch, :].T[HEAD_DIM:, :].astype(bf16)

    cos = cq_ref[...]
    sin = sq_ref[...]
    lane = lax.broadcasted_iota(jnp.int32, (QB, LANES), 1)
    half_id = lane // HEAD_DIM

    def stack_heads(x, n):
        parts = [jnp.where(half_id == (h % 2), x[:, (h // 2) * LANES:(h // 2 + 1) * LANES], 0.0)
                 for h in range(n)]
        return jnp.concatenate(parts, axis=0).astype(bf16)

    iq_stack = stack_heads(_rope(iq_ref[0], cos, sin), N_IDX)
    iw_t = (iw_ref[0] * IDX_SCALE).T
    q_pos = lax.broadcasted_iota(jnp.int32, (ch, QB), 1) + qi * QB
    k_row = lax.broadcasted_iota(jnp.int32, (ch, QB), 0)

    def score_chunk(c, carry):
        r0 = pl.multiple_of(c * ch, ch)
        rel = _dot_nt(ki2_ref[pl.ds(r0, ch), :], iq_stack)
        score = jnp.zeros((ch, QB), f32)
        for j in range(N_IDX):
            score = score + jnp.maximum(rel[:, j * QB:(j + 1) * QB], 0.0) * iw_t[j:j + 1, :]
        score = jnp.where(k_row + r0 <= q_pos, score + 0.0, NEG)
        key_ref[pl.ds(r0, ch), :] = _float_key(score)
        return carry

    lax.fori_loop(0, nch, score_chunk, 0)

    kf = float(k_top)

    sel = DSA_SEL
    nsel = (qi * QB + QB + sel - 1) // sel

    def fold_rows(x):
        return jnp.sum(x.reshape(sel // 8, 8, QB), axis=0)

    def bisect(n):
        def count_where(pred):
            acc = jnp.zeros((8, QB), f32)
            for c in range(n):
                acc = acc + fold_rows(jnp.where(pred(key_ref[c * sel:(c + 1) * sel, :]), 1.0, 0.0))
            return jnp.sum(acc, axis=0, keepdims=True)

        def run():
            zero_key = jnp.zeros((1, QB), jnp.int32)
            lo = jnp.where(count_where(lambda k: k >= zero_key) >= kf,
                           jnp.int32(0), jnp.int32(_INT_MIN))

            def bit_step(i, lo):
                trial = lo | jnp.left_shift(jnp.int32(1), 30 - i)
                return jnp.where(count_where(lambda k: k >= trial) >= kf, trial, lo)

            thr = lax.fori_loop(0, 31, bit_step, lo)
            return thr, kf - count_where(lambda k: k > thr), count_where(lambda k: k == thr)
        return run

    thr, need, n_eq = lax.switch(nsel - 1, [bisect(n) for n in range(1, t // sel + 1)])
    thr_adm = jnp.maximum(thr, _KEY_HALF_NEG + 1)

    def bias_chunk(c, carry):
        r0 = pl.multiple_of(c * ch, ch)
        bias_ref[pl.ds(r0, ch), :] = jnp.where(key_ref[pl.ds(r0, ch), :] >= thr_adm, 0.0, NEG)
        return carry

    lax.fori_loop(0, nch, bias_chunk, 0)

    tie_overflow = jnp.where((n_eq > need) & (thr > _KEY_HALF_NEG), 1.0, 0.0)

    @pl.when(jnp.max(tie_overflow) > 0.0)
    def _():
        i = lax.broadcasted_iota(jnp.int32, (ch, ch), 0)
        j = lax.broadcasted_iota(jnp.int32, (ch, ch), 1)
        earlier = jnp.where(j < i, 1.0, 0.0).astype(bf16)

        def body(c, seen):
            r0 = pl.multiple_of(c * ch, ch)
            kc = key_ref[pl.ds(r0, ch), :]
            eq = jnp.where(kc == thr, 1.0, 0.0)
            rank = _dot(earlier, eq.astype(bf16)) + seen
            tie_bias = jnp.where(rank < need, 0.0, NEG)
            bias_ref[pl.ds(r0, ch), :] = jnp.where(
                kc >= thr_adm, jnp.where(kc > thr, 0.0, tie_bias), NEG)
            return seen + jnp.sum(eq, axis=0, keepdims=True)

        lax.fori_loop(0, nch, body, jnp.zeros((1, QB), f32))

    x = dq_ref[0]
    ms = _dot3_right(x * x, _group_mean_matrix(W_DSA, HEAD_DIM))
    qn = x * lax.rsqrt(ms + EPS) * qn_ref[...]
    qn = _rope(qn, cos[:, :W_DSA], sin[:, :W_DSA]) * (ATT_SCALE * LOG2_E)
    q_stack = stack_heads(qn, N_DSA)
    wide = N_DSA * QB

    def att_chunk(c, carry):
        m, l, acc = carry
        r0 = pl.multiple_of(c * ch, ch)
        logits = _dot_nt(kn2_ref[pl.ds(r0, ch), :], q_stack)
        logits = logits + jnp.concatenate([bias_ref[pl.ds(r0, ch), :]] * N_DSA, axis=1)
        m_new = jnp.maximum(m, jnp.max(logits, axis=0, keepdims=True))
        alpha = jnp.exp2(m - m_new)
        p = jnp.exp2(logits - m_new)
        l = alpha * l + jnp.sum(p, axis=0, keepdims=True)
        acc = alpha * acc + _dot(vt_ref[c], p.astype(bf16))
        return m_new, l, acc

    init = (jnp.full((1, wide), SOFTMAX_MAX_FLOOR, f32), jnp.zeros((1, wide), f32),
            jnp.zeros((HEAD_DIM, wide), f32))
    _, l, acc = lax.fori_loop(0, nch, att_chunk, init)
    out_t = acc / l
    for pair in range(N_DSA // 2):
        both = jnp.concatenate([out_t[:, (2 * pair) * QB:(2 * pair + 1) * QB],
                                out_t[:, (2 * pair + 1) * QB:(2 * pair + 2) * QB]], axis=0)
        o_ref[0, :, pair * LANES:(pair + 1) * LANES] = both.T.astype(o_ref.dtype)


def _dsa_attention(p3, cos_t, sin_t, qn_gain, kn_gain):
    b, t, _ = p3.shape
    k_top = min(TOPK_MAX, t // 4)
    return pl.pallas_call(
        functools.partial(_dsa_kernel, k_top=k_top),
        grid=(b, t // QB),
        in_specs=[
            pl.BlockSpec((1, QB, W_DSA), lambda bi, qi: (bi, qi, COL_DQ // W_DSA)),
            pl.BlockSpec((1, t, LANES), lambda bi, qi: (bi, 0, COL_DKV // LANES)),
            pl.BlockSpec((1, QB, N_IDX * IDX_DIM), lambda bi, qi: (bi, qi, COL_IQ // (N_IDX * IDX_DIM))),
            pl.BlockSpec((1, t, LANES), lambda bi, qi: (bi, 0, COL_IK // LANES)),
            pl.BlockSpec((1, QB, LANES), lambda bi, qi: (bi, qi, COL_IW // LANES)),
            pl.BlockSpec((QB, 4 * LANES), lambda bi, qi: (qi, 0)),
            pl.BlockSpec((QB, 4 * LANES), lambda bi, qi: (qi, 0)),
            pl.BlockSpec((t, LANES), lambda bi, qi: (0, 0)),
            pl.BlockSpec((t, LANES), lambda bi, qi: (0, 0)),
            pl.BlockSpec((1, W_DSA), lambda bi, qi: (0, 0)),
            pl.BlockSpec((1, LANES), lambda bi, qi: (0, 0)),
        ],
        out_specs=pl.BlockSpec((1, QB, W_DSA), lambda bi, qi: (bi, qi, 0)),
        out_shape=jax.ShapeDtypeStruct((b, t, W_DSA), bf16),
        scratch_shapes=[pltpu.VMEM((t, LANES), bf16), pltpu.VMEM((t // DSA_CH, HEAD_DIM, DSA_CH), bf16),
                        pltpu.VMEM((t, LANES), bf16), pltpu.VMEM((t, QB), jnp.int32),
                        pltpu.VMEM((t, QB), f32)],
        compiler_params=pltpu.CompilerParams(
            dimension_semantics=("parallel", "arbitrary"), vmem_limit_bytes=48 * 2 ** 20),
        name="dsa_attention",
    )(p3, p3, p3, p3, p3, cos_t, sin_t, cos_t, sin_t, qn_gain, kn_gain)


def _hgrn_kernel(hq_ref, hf_ref, hi_ref, vt_ref, lb_ref, o_ref, st_ref):
    @pl.when(pl.program_id(1) == 0)
    def _():
        st_ref[...] = jnp.zeros_like(st_ref)

    c = HG_CHUNK
    nsub = c // HG_SUB
    ti = lax.broadcasted_iota(jnp.int32, (c, c), 0)
    si = lax.broadcasted_iota(jnp.int32, (c, c), 1)
    cum = jnp.where(si <= ti, 1.0, 0.0).astype(bf16)
    same_sub = (ti // HG_SUB) == (si // HG_SUB)
    later_sub = (ti // HG_SUB) > (si // HG_SUB)
    delta = ti - si
    rsub = lax.broadcasted_iota(jnp.int32, (c, HG_DK), 0) // HG_SUB

    for sub in range(HG_ROWS // c):
        r0 = sub * c
        for h in range(N_HG):
            qraw = hq_ref[0, r0:r0 + c, h * HG_DK:(h + 1) * HG_DK]
            fpre = hf_ref[0, r0:r0 + c, h * HG_DK:(h + 1) * HG_DK]
            lb = lb_ref[:, h * HG_DK:(h + 1) * HG_DK]
            v = hi_ref[0, r0:r0 + c, h * HG_DV:(h + 1) * HG_DV]
            vt = vt_ref[0, h, sub]
            q = qraw * _sigmoid(qraw)
            f = lb + (1.0 - lb) * _sigmoid(fpre)
            g = jnp.log(jnp.maximum(f, F_MIN))
            kk = 1.0 - f
            b = _dot3_left(cum, g)
            b_last = b[c - 1:c, :]
            ends = [b[(j + 1) * HG_SUB - 1:(j + 1) * HG_SUB, :] for j in range(nsub)]
            b_end = jnp.concatenate([jnp.broadcast_to(e, (HG_SUB, HG_DK)) for e in ends], axis=0)

            k_rel = kk * jnp.exp(b_end - b)
            k_cat = jnp.concatenate([jnp.where(rsub == j, k_rel, 0.0) for j in range(nsub - 1)], axis=1)
            q_cat = jnp.concatenate([q * jnp.exp(jnp.minimum(b - ends[j], 0.0)) for j in range(nsub - 1)],
                                    axis=1)
            att = jnp.where(later_sub, _dot_nt(q_cat.astype(bf16), k_cat.astype(bf16)), 0.0)

            for d in range(HG_SUB):
                if d == 0:
                    term = q * kk
                else:
                    kd = pltpu.roll(kk, d, axis=0)
                    bd = pltpu.roll(b, d, axis=0)
                    term = q * kd * jnp.exp(jnp.minimum(b - bd, 0.0))
                s = jnp.sum(term, axis=-1, keepdims=True)
                att = att + jnp.where(same_sub & (delta == d), s, 0.0)

            st = st_ref[h]
            o = _dot(att.astype(bf16), v.astype(bf16)) + _dot_nt((q * jnp.exp(b)).astype(bf16), st.astype(bf16))
            o_ref[0, h, r0:r0 + c, :] = o
            k_hat = kk * jnp.exp(b_last - b)
            st_ref[h] = st * jnp.exp(b_last) + _dot(vt.astype(bf16), k_hat.astype(bf16))


def _hgrn(p3, vt, lb):
    b, t, _ = p3.shape
    per = HG_ROWS // HG_CHUNK
    return pl.pallas_call(
        _hgrn_kernel,
        grid=(b, t // HG_ROWS),
        in_specs=[pl.BlockSpec((1, HG_ROWS, W_HF), lambda bi, i: (bi, i, COL_HQ // W_HF)),
                  pl.BlockSpec((1, HG_ROWS, W_HF), lambda bi, i: (bi, i, COL_HF // W_HF)),
                  pl.BlockSpec((1, HG_ROWS, W_HV), lambda bi, i: (bi, i, COL_HI // W_HV)),
                  pl.BlockSpec((1, N_HG, per, HG_DV, HG_CHUNK), lambda bi, i: (bi, 0, i, 0, 0)),
                  pl.BlockSpec((1, W_HF), lambda bi, i: (0, 0))],
        out_specs=pl.BlockSpec((1, N_HG, HG_ROWS, HG_DV), lambda bi, i: (bi, 0, i, 0)),
        out_shape=jax.ShapeDtypeStruct((b, N_HG, t, HG_DV), f32),
        scratch_shapes=[pltpu.VMEM((N_HG, HG_DV, HG_DK), f32)],
        compiler_params=pltpu.CompilerParams(dimension_semantics=("parallel", "arbitrary")),
        name="hgrn2",
    )(p3, p3, p3, vt, lb)


def _merge_kernel(x_ref, sb_ref, dsa_ref, hg_ref, gs_ref, gd_ref, gh_ref, og_ref, on_ref,
                  wsb_ref, wdsa_ref, whg_ref, wout_ref, o_ref):
    o = hg_ref[...]
    ms = _dot3_right(o * o, _group_mean_matrix(W_HV, HG_DV))
    gate = og_ref[...]
    og = o * lax.rsqrt(ms + EPS) * on_ref[...] * (gate * _sigmoid(gate))
    y_hg = _dot(og.astype(bf16), whg_ref[...])
    y_sb = _dot(sb_ref[...], wsb_ref[...])
    y_dsa = _dot(dsa_ref[...], wdsa_ref[...])
    mixed = (_sigmoid(gs_ref[...]) * y_sb + _sigmoid(gd_ref[...]) * y_dsa + _sigmoid(gh_ref[...]) * y_hg)
    o_ref[...] = x_ref[...] + _dot(mixed.astype(bf16), wout_ref[...])


def _merge(x2d, att_sb, att_dsa, o_hg, p2d, onorm, w_sb, w_dsa, w_hg, w_out):
    m, d = x2d.shape
    tm = 512
    row = lambda w: pl.BlockSpec((tm, w), lambda i: (i, 0))
    full = lambda a: pl.BlockSpec(a.shape, lambda i: (0, 0))
    return pl.pallas_call(
        _merge_kernel,
        grid=(m // tm,),
        in_specs=[row(d), row(W_SB), row(W_DSA), row(W_HV),
                  pl.BlockSpec((tm, d), lambda i: (i, COL_G // d)),
                  pl.BlockSpec((tm, d), lambda i: (i, COL_G // d + 1)),
                  pl.BlockSpec((tm, d), lambda i: (i, COL_G // d + 2)),
                  pl.BlockSpec((tm, W_HV), lambda i: (i, COL_HG // W_HV)),
                  full(onorm), full(w_sb), full(w_dsa), full(w_hg), full(w_out)],
        out_specs=row(d),
        out_shape=jax.ShapeDtypeStruct((m, d), f32),
        compiler_params=pltpu.CompilerParams(
            dimension_semantics=("parallel",), vmem_limit_bytes=48 * 2 ** 20),
        name="merge_outproj",
    )(x2d, att_sb, att_dsa, o_hg, p2d, p2d, p2d, p2d, onorm, w_sb, w_dsa, w_hg, w_out)


def _mlp_kernel(x_ref, g_ref, wu_ref, wd_ref, o_ref, h_ref, acc_ref):
    j = pl.program_id(1)

    @pl.when(j == 0)
    def _():
        x = x_ref[...]
        ms = jnp.mean(x * x, axis=-1, keepdims=True)
        h_ref[...] = (x * lax.rsqrt(ms + EPS) * g_ref[...]).astype(bf16)
        acc_ref[...] = jnp.zeros_like(acc_ref)

    u = jnp.maximum(_dot(h_ref[...], wu_ref[...]), 0.0)
    acc_ref[...] += _dot((u * u).astype(bf16), wd_ref[...])

    @pl.when(j == pl.num_programs(1) - 1)
    def _():
        o_ref[...] = x_ref[...] + acc_ref[...]


def _mlp(x2d, gain, w_up, w_down):
    m, d = x2d.shape
    dff = w_up.shape[1]
    tm, tf = 1024, 512
    return pl.pallas_call(
        _mlp_kernel,
        grid=(m // tm, dff // tf),
        in_specs=[pl.BlockSpec((tm, d), lambda i, j: (i, 0)),
                  pl.BlockSpec((1, d), lambda i, j: (0, 0)),
                  pl.BlockSpec((d, tf), lambda i, j: (0, j)),
                  pl.BlockSpec((tf, d), lambda i, j: (j, 0))],
        out_specs=pl.BlockSpec((tm, d), lambda i, j: (i, 0)),
        out_shape=jax.ShapeDtypeStruct((m, d), f32),
        scratch_shapes=[pltpu.VMEM((tm, d), bf16), pltpu.VMEM((tm, d), f32)],
        compiler_params=pltpu.CompilerParams(
            dimension_semantics=("parallel", "arbitrary"), vmem_limit_bytes=48 * 2 ** 20),
        name="mlp",
    )(x2d, gain, w_up, w_down)


def _reorder_w_in(w):
    widths = (W_SB, W_SB, W_SB, W_DSA, HEAD_DIM, HEAD_DIM, N_IDX * IDX_DIM, IDX_DIM, N_IDX,
              W_HF, W_HF, W_HV, W_HV, w.shape[0], w.shape[0], w.shape[0])
    pts = np.concatenate([[0], np.cumsum(widths)])
    seg = {n: w[:, int(pts[i]):int(pts[i + 1])] for i, n in enumerate(_IN_NAMES)}
    zeros = lambda n: jnp.zeros((w.shape[0], n), w.dtype)
    cols = [seg["g_sb"], seg["g_dsa"], seg["g_hg"], seg["hq"], seg["hf"], seg["iq"], seg["hi"], seg["hg"],
            seg["dk"], seg["dv"], seg["ik"], zeros(LANES - IDX_DIM),
            seg["sq"], seg["sk"], seg["sv"], seg["dq"], seg["iw"], zeros(LANES - N_IDX),
            zeros(NP - COL_IW - LANES)]
    out = jnp.concatenate(cols, axis=1).astype(bf16)
    assert out.shape[1] == NP
    return out


def _rope_tables(t):
    half = ROT_DIM // 2
    inv = ROPE_THETA ** (-(np.arange(half, dtype=np.float32) * 2.0) / ROT_DIM)
    ang = jnp.arange(t, dtype=f32)[:, None] * jnp.asarray(inv, f32)[None, :]
    cos, sin = jnp.cos(ang), jnp.sin(ang)
    pad = HEAD_DIM - ROT_DIM
    cos_h = jnp.concatenate([cos, cos, jnp.ones((t, pad), f32)], axis=1)
    sin_h = jnp.concatenate([-sin, sin, jnp.zeros((t, pad), f32)], axis=1)
    reps = 4 * LANES // HEAD_DIM
    return jnp.tile(cos_h, (1, reps)), jnp.tile(sin_h, (1, reps))


def kernel(x, norm_mix, w_in, qn_dsa, kn_dsa, hgrn_lb, hgrn_onorm, w_br_sb, w_br_dsa, w_br_hgrn,
           w_out, norm_mlp, w_up, w_down):
    bsz, t, d = x.shape
    depth = w_in.shape[0]
    m = bsz * t
    assert d == 1024 and t % QB == 0 and m % 1024 == 0

    p_lb = jax.nn.softmax(hgrn_lb.astype(f32), axis=0)
    lbs = jnp.cumsum(p_lb, axis=0) - p_lb[0:1]
    cos_t, sin_t = _rope_tables(t)

    xf = x.reshape(m, d)
    for l in range(depth):
        p2d = _inproj(xf, norm_mix[l][None, :], _reorder_w_in(w_in[l]))
        p3 = p2d.reshape(bsz, t, NP)
        att_sb = _sb_attention(p3)
        att_dsa = _dsa_attention(p3, cos_t, sin_t, jnp.tile(qn_dsa[l], N_DSA)[None, :],
                                 jnp.tile(kn_dsa[l], 2)[None, :])
        hi = p3[:, :, COL_HI:COL_HI + W_HV]
        vt = jnp.transpose(hi.reshape(bsz, t // HG_CHUNK, HG_CHUNK, N_HG, HG_DV), (0, 3, 1, 4, 2))
        o_hg = _hgrn(p3, vt, lbs[l][None, :])
        o_hg = jnp.transpose(o_hg, (0, 2, 1, 3)).reshape(m, W_HV)
        xf = _merge(xf, att_sb.reshape(m, W_SB), att_dsa.reshape(m, W_DSA), o_hg, p2d,
                    jnp.tile(hgrn_onorm[l], N_HG)[None, :],
                    w_br_sb[l].astype(bf16), w_br_dsa[l].astype(bf16), w_br_hgrn[l].astype(bf16),
                    w_out[l].astype(bf16))
        xf = _mlp(xf, norm_mlp[l][None, :], w_up[l].astype(bf16), w_down[l].astype(bf16))
    return xf.reshape(bsz, t, d)
```
